```python
import jax, jax.numpy as jnp
from jax import lax
import numpy as np

D_MODEL = 1024
BATCH = 8
SEQ = 4096
DEPTH = 4

CHUNK = 64
N_META = 16
EPS = 1e-6
GLA_HEADS = 4
GLA_DV = D_MODEL // 2 // GLA_HEADS
GLA_DK = GLA_DV // 2
GLA_LOWRANK = 16
GLA_TAU = 16.0
FOX_DH = 64
FOX_HEADS = D_MODEL // 2 // FOX_DH
Q_BLOCK = 128
FORGET_BIAS_INIT = 4.0
D_FF = 128 * (-(-8 * D_MODEL // (3 * 128)))
CONV_W = 3
GLA_WIDTH = GLA_HEADS * GLA_DV
FOX_WIDTH = FOX_HEADS * FOX_DH
MIX_WIDTH = GLA_WIDTH + FOX_WIDTH
IN_SIZES = (GLA_HEADS * GLA_DK, GLA_HEADS * GLA_DK, GLA_WIDTH, GLA_WIDTH, GLA_LOWRANK,
            FOX_WIDTH, FOX_WIDTH, FOX_WIDTH, FOX_HEADS)
N_IN = sum(IN_SIZES)

kernel_name = "hymba_style_gla_fox_convffn_streaming"


def rms_norm(x, g):
    xf = x.astype(jnp.float32)
    y = xf * lax.rsqrt(jnp.mean(xf * xf, axis=-1, keepdims=True) + EPS)
    return (y * g.astype(jnp.float32)).astype(x.dtype)


def chunk_kv_summary(k, v, log_a):
    log_b = jnp.cumsum(log_a, axis=3)
    log_end = log_b[:, :, :, -1:, :]
    u = jnp.einsum('bhntk,bhntv->bhnkv', k * jnp.exp(log_end - log_b), v)
    return u, jnp.exp(log_end[:, :, :, 0, :])


def gla_chunk_causal(q, k, v, log_a):
    B, L, H, _ = q.shape
    dv = v.shape[-1]
    n_chunks = (L - N_META) // CHUNK

    def split(t):
        t = jnp.transpose(t.astype(jnp.float32), (0, 2, 1, 3))
        d = t.shape[-1]
        return (t[:, :, :N_META].reshape(B, H, 1, N_META, d),
                t[:, :, N_META:].reshape(B, H, n_chunks, CHUNK, d))

    q_m, q_r = split(q)
    k_m, k_r = split(k)
    v_m, v_r = split(v)
    a_m, a_r = split(log_a)
    u_m, _ = chunk_kv_summary(k_m, v_m, a_m)
    s_meta = u_m[:, :, 0]
    u_r, decay_r = chunk_kv_summary(k_r, v_r, a_r)

    def step(s, inp):
        u_c, d_c = inp
        s = d_c[..., None] * s + u_c
        return s, s

    _, states = lax.scan(step, s_meta, (jnp.moveaxis(u_r, 2, 0), jnp.moveaxis(decay_r, 2, 0)))
    states = jnp.moveaxis(states, 0, 2)
    o_m = jnp.einsum('bhtk,bhkv->bhtv', q_m[:, :, 0], s_meta)
    o_r = jnp.einsum('bhntk,bhnkv->bhntv', q_r, states).reshape(B, H, n_chunks * CHUNK, dv)
    o = jnp.concatenate([o_m, o_r], axis=2)
    return jnp.transpose(o, (0, 2, 1, 3))


def forgetting_attention(q, k, v, log_f):
    B, L, H, dh = q.shape
    n_blk = -(-L // Q_BLOCK)
    pad = n_blk * Q_BLOCK - L
    c = jnp.cumsum(log_f.astype(jnp.float32), axis=1)
    c_k = jnp.transpose(c, (0, 2, 1))
    c_q = jnp.pad(c, ((0, 0), (0, pad), (0, 0)), mode='edge')
    q_pad = jnp.pad(q.astype(jnp.float32) * dh ** -0.5, ((0, 0), (0, pad), (0, 0), (0, 0)))
    q_blocks = jnp.moveaxis(q_pad.reshape(B, n_blk, Q_BLOCK, H, dh), 1, 0)
    c_blocks = jnp.moveaxis(jnp.transpose(c_q.reshape(B, n_blk, Q_BLOCK, H), (0, 1, 3, 2)), 1, 0)
    kf = k.astype(jnp.float32)
    vf = v.astype(jnp.float32)
    key_pos = jnp.arange(L)

    def block(inp):
        q_b, c_b, start = inp
        logits = (jnp.einsum('bqhd,bkhd->bhqk', q_b, kf)
                  + c_b[..., None] - c_k[:, :, None, :])
        q_pos = start + jnp.arange(Q_BLOCK)
        logits = jnp.where(key_pos[None, :] <= q_pos[:, None], logits, -jnp.inf)
        return jnp.einsum('bhqk,bkhd->bqhd', jax.nn.softmax(logits, axis=-1), vf)

    out = lax.map(block, (q_blocks, c_blocks, jnp.arange(n_blk) * Q_BLOCK))
    out = jnp.moveaxis(out, 0, 1).reshape(B, n_blk * Q_BLOCK, H, dh)
    return out[:, :L]


def conv_ffn(x, w_up, conv_w, conv_b, w_down):
    h = x @ w_up
    L = h.shape[1]
    hp = jnp.pad(h, ((0, 0), (CONV_W - 1, 0), (0, 0)))
    hc = conv_b
    for j in range(CONV_W):
        hc = hc + hp[:, j:j + L] * conv_w[j]
    u, g = jnp.split(hc, 2, axis=-1)
    return (jax.nn.silu(g) * u) @ w_down


def setup_inputs(seed: int = 0) -> dict:
    key = jax.random.key(seed)
    ks = jax.random.split(key, 16)

    def nrm(k, shape, scale):
        return jax.random.normal(k, shape, jnp.float32) * scale

    return {
        'x': nrm(ks[0], (BATCH, SEQ, D_MODEL), 1.0),
        'meta_tokens': nrm(ks[1], (N_META, D_MODEL), 1.0),
        'attn_norm': 1.0 + nrm(ks[2], (DEPTH, D_MODEL), 0.02),
        'w_in': nrm(ks[3], (DEPTH, D_MODEL, N_IN), D_MODEL ** -0.5),
        'w_alpha_up': nrm(ks[4], (DEPTH, GLA_LOWRANK, GLA_HEADS * GLA_DK), GLA_LOWRANK ** -0.5),
        'b_alpha': nrm(ks[5], (DEPTH, GLA_HEADS * GLA_DK), 0.1),
        'b_forget': FORGET_BIAS_INIT + nrm(ks[6], (DEPTH, FOX_HEADS), 0.5),
        'gla_norm': 1.0 + nrm(ks[7], (DEPTH, GLA_WIDTH), 0.02),
        'fox_norm': 1.0 + nrm(ks[8], (DEPTH, FOX_WIDTH), 0.02),
        'w_out': nrm(ks[9], (DEPTH, MIX_WIDTH, D_MODEL), MIX_WIDTH ** -0.5),
        'ffn_norm': 1.0 + nrm(ks[10], (DEPTH, D_MODEL), 0.02),
        'w_up': nrm(ks[11], (DEPTH, D_MODEL, 2 * D_FF), D_MODEL ** -0.5),
        'conv_w': nrm(ks[12], (DEPTH, CONV_W, 2 * D_FF), CONV_W ** -0.5),
        'conv_b': nrm(ks[13], (DEPTH, 2 * D_FF), 0.02),
        'w_down': nrm(ks[14], (DEPTH, D_FF, D_MODEL), D_FF ** -0.5),
        'final_norm': 1.0 + nrm(ks[15], (D_MODEL,), 0.02),
    }


def reference(x, meta_tokens, attn_norm, w_in, w_alpha_up, b_alpha, b_forget, gla_norm,
              fox_norm, w_out, ffn_norm, w_up, conv_w, conv_b, w_down, final_norm):
    B = x.shape[0]
    meta = jnp.broadcast_to(meta_tokens[None].astype(x.dtype), (B, N_META, D_MODEL))
    h = jnp.concatenate([meta, x], axis=1)
    L = h.shape[1]
    cuts = np.cumsum(IN_SIZES)[:-1].tolist()
    for i in range(DEPTH):
        xn = rms_norm(h, attn_norm[i])
        gq, gk, gv, gr, glr, fq, fk, fv, ff = jnp.split(xn @ w_in[i], cuts, axis=-1)
        log_a = jax.nn.log_sigmoid((glr @ w_alpha_up[i] + b_alpha[i]).astype(jnp.float32)) / GLA_TAU
        o_gla = gla_chunk_causal(gq.reshape(B, L, GLA_HEADS, GLA_DK) * GLA_DK ** -0.5,
                                 gk.reshape(B, L, GLA_HEADS, GLA_DK),
                                 gv.reshape(B, L, GLA_HEADS, GLA_DV),
                                 log_a.reshape(B, L, GLA_HEADS, GLA_DK)).astype(h.dtype)
        o_gla = rms_norm(o_gla, gla_norm[i].reshape(GLA_HEADS, GLA_DV)).reshape(B, L, GLA_WIDTH)
        o_gla = o_gla * jax.nn.silu(gr)
        log_f = jax.nn.log_sigmoid((ff + b_forget[i]).astype(jnp.float32))
        o_fox = forgetting_attention(fq.reshape(B, L, FOX_HEADS, FOX_DH),
                                     fk.reshape(B, L, FOX_HEADS, FOX_DH),
                                     fv.reshape(B, L, FOX_HEADS, FOX_DH), log_f).astype(h.dtype)
        o_fox = rms_norm(o_fox, fox_norm[i].reshape(FOX_HEADS, FOX_DH)).reshape(B, L, FOX_WIDTH)
        h = h + jnp.concatenate([o_gla, o_fox], axis=-1) @ w_out[i]
        h = h + conv_ffn(rms_norm(h, ffn_norm[i]), w_up[i], conv_w[i], conv_b[i], w_down[i])
    return rms_norm(h, final_norm)[:, N_META:]
```

```python
import functools

import jax
import jax.numpy as jnp
import numpy as np
from jax import lax
from jax.experimental import pallas as pl
from jax.experimental.pallas import tpu as pltpu

D_MODEL = 1024
N_META = 16
CHUNK = 64
EPS = 1e-6
GLA_HEADS = 4
GLA_DK = 64
GLA_DV = 128
GLA_LOWRANK = 16
GLA_TAU = 16.0
GLA_QK = GLA_HEADS * GLA_DK
GLA_WIDTH = GLA_HEADS * GLA_DV
FOX_HEADS = 8
FOX_DH = 64
FOX_WIDTH = FOX_HEADS * FOX_DH
D_FF = 2816
CONV_W = 3

LANE = 128
SUBLANE = 8
FRONT = 128
N_DUMMY = FRONT - N_META
TM = 384
TQ = 384
FF_CHUNK = 256
N_FF_CHUNKS = D_FF // FF_CHUNK
LOG2E = 1.4426950408889634
KEY_MASK = 16384.0
NEG = -1e30
VMEM_LIMIT = 56 * 1024 * 1024

_C_GLA = 0
_C_FQ = 1536
_C_FK = 2048
_C_FV = 2560
_C_SM = 3072
_N_MAIN = 3200

F32 = jnp.float32
BF16 = jnp.bfloat16


def _dot(a, b):
    return jnp.dot(a, b, preferred_element_type=F32)


def _log_sigmoid(x):
    return jnp.minimum(x, 0.0) - jnp.log1p(jnp.exp(-jnp.abs(x)))


def _split3(x):
    p1 = x.astype(BF16).astype(F32)
    r1 = x - p1
    p2 = r1.astype(BF16).astype(F32)
    p3 = (r1 - p2).astype(BF16).astype(F32)
    return p1, p2, p3


def _tri_sum(tri, x):
    p1, p2, p3 = _split3(x)
    return (_dot(tri, p1.astype(BF16)) + _dot(tri, p2.astype(BF16))) + _dot(tri, p3.astype(BF16))


def _inproj_kernel(h_ref, g_ref, w_ref, wa_ref, ba_ref, bf_ref, sq_ref, sk_ref, tri_ref,
                   gla_ref, loga_ref, qa_ref, ka_ref, va_ref, carry_ref):
    i = pl.program_id(1)
    tm = h_ref.shape[1]
    x = h_ref[0]
    ms = jnp.mean(x * x, axis=-1, keepdims=True)
    xn = (x * lax.rsqrt(ms + EPS) * g_ref[...]).astype(BF16)

    gla_ref[0] = _dot(xn, w_ref[:, _C_GLA:_C_FQ])
    fq = _dot(xn, w_ref[:, _C_FQ:_C_FK])
    fk = _dot(xn, w_ref[:, _C_FK:_C_FV])
    fv = _dot(xn, w_ref[:, _C_FV:_C_SM])
    sm = _dot(xn, w_ref[:, _C_SM:_N_MAIN])

    z = _dot(sm.astype(BF16), wa_ref[...]) + ba_ref[...]
    loga_ref[0] = _log_sigmoid(z) * (1.0 / GLA_TAU)

    lane = lax.broadcasted_iota(jnp.int32, (tm, LANE), 1)
    row = lax.broadcasted_iota(jnp.int32, (tm, LANE), 0) + i * tm
    real = row >= N_DUMMY
    head_lane = lane < FOX_HEADS
    lf = jnp.where(head_lane & real, _log_sigmoid(sm + bf_ref[...]), 0.0)

    @pl.when(i == 0)
    def _():
        carry_ref[...] = jnp.zeros_like(carry_ref)

    carry = carry_ref[0:1, :]
    tri = tri_ref[...]
    blocks = []
    for j in range(tm // LANE):
        cs = _tri_sum(tri, lf[j * LANE:(j + 1) * LANE]) + carry
        carry = cs[LANE - 1:LANE, :]
        blocks.append(cs)
    carry_ref[...] = jnp.broadcast_to(carry, carry_ref.shape)
    c = jnp.concatenate(blocks, axis=0) * LOG2E
    c = jnp.where(head_lane, jnp.where(real, c, KEY_MASK), 0.0)
    c1, c2, c3 = _split3(c)
    cc = (c1 + pltpu.roll(c2, FOX_HEADS, axis=1)) + (pltpu.roll(c3, 2 * FOX_HEADS, axis=1)
                                                      + jnp.where(lane == 3 * FOX_HEADS, 1.0, 0.0))
    cc = cc.astype(BF16)
    augq = _dot(cc, sq_ref[...])
    augk = _dot(cc, sk_ref[...])

    low = lane < FOX_DH
    ones_col = jnp.where(lane == FOX_DH, 1.0, 0.0)
    for h in range(FOX_HEADS):
        pair = slice((h // 2) * LANE, (h // 2 + 1) * LANE)
        blk = slice(h * LANE, (h + 1) * LANE)
        xq, xk, xv = fq[:, pair], fk[:, pair], fv[:, pair]
        if h % 2:
            xq = pltpu.roll(xq, FOX_DH, axis=1)
            xk = pltpu.roll(xk, FOX_DH, axis=1)
            xv = pltpu.roll(xv, FOX_DH, axis=1)
        qa_ref[0, :, blk] = jnp.where(low, xq * (LOG2E * FOX_DH ** -0.5), augq[:, blk]).astype(BF16)
        ka_ref[0, :, blk] = jnp.where(low, xk, augk[:, blk]).astype(BF16)
        va_ref[0, :, blk] = jnp.where(low, xv, ones_col).astype(BF16)


def _inproj(h, g, w_main, wa, ba, bf, sq, sk, tri):
    B, LP, D = h.shape
    nt = LP // TM
    const = lambda shape: pl.BlockSpec(shape, lambda b, i: (0,) * len(shape))
    row_blk = lambda n: pl.BlockSpec((1, TM, n), lambda b, i: (b, i, 0))
    return pl.pallas_call(
        _inproj_kernel,
        grid=(B, nt),
        in_specs=[row_blk(D), const((1, D)), const((D, _N_MAIN)), const((LANE, GLA_QK)),
                  const((1, GLA_QK)), const((1, LANE)), const((LANE, FOX_HEADS * LANE)),
                  const((LANE, FOX_HEADS * LANE)), const((LANE, LANE))],
        out_specs=[row_blk(_C_FQ), row_blk(GLA_QK), row_blk(FOX_HEADS * LANE),
                   row_blk(FOX_HEADS * LANE), row_blk(FOX_HEADS * LANE)],
        out_shape=[jax.ShapeDtypeStruct((B, LP, _C_FQ), F32),
                   jax.ShapeDtypeStruct((B, LP, GLA_QK), F32),
                   jax.ShapeDtypeStruct((B, LP, FOX_HEADS * LANE), BF16),
                   jax.ShapeDtypeStruct((B, LP, FOX_HEADS * LANE), BF16),
                   jax.ShapeDtypeStruct((B, LP, FOX_HEADS * LANE), BF16)],
        scratch_shapes=[pltpu.VMEM((SUBLANE, LANE), F32)],
        compiler_params=pltpu.CompilerParams(
            dimension_semantics=("parallel", "arbitrary"), vmem_limit_bytes=VMEM_LIMIT),
        name="inproj",
    )(h, g, w_main, wa, ba, bf, sq, sk, tri)


def _gla_kernel(gla_ref, loga_ref, gn_ref, tri_ref, o_ref, state_ref):
    i = pl.program_id(1)
    tm = gla_ref.shape[1]

    @pl.when(i == 0)
    def _():
        state_ref[...] = jnp.zeros_like(state_ref)

    tri = tri_ref[...]
    srow = lax.broadcasted_iota(jnp.int32, (GLA_WIDTH, GLA_QK), 0) // GLA_DV
    scol = lax.broadcasted_iota(jnp.int32, (GLA_WIDTH, GLA_QK), 1) // GLA_DK
    same_head = srow == scol
    gn = gn_ref[...]
    for s in range(tm // CHUNK):
        rows = slice(s * CHUNK, (s + 1) * CHUNK)
        row = lax.broadcasted_iota(jnp.int32, (CHUNK, 1), 0) + (i * tm + s * CHUNK)
        real = row >= N_DUMMY
        q = gla_ref[0, rows, 0:GLA_QK]
        k = jnp.where(real, gla_ref[0, rows, GLA_QK:2 * GLA_QK], 0.0)
        v = gla_ref[0, rows, 2 * GLA_QK:2 * GLA_QK + GLA_WIDTH]
        r = gla_ref[0, rows, 2 * GLA_QK + GLA_WIDTH:2 * GLA_QK + 2 * GLA_WIDTH]
        la = jnp.where(real, loga_ref[0, rows, :], 0.0)
        sfx = _tri_sum(tri, la)
        kw = (k * jnp.exp(sfx - la)).astype(BF16)
        decay = jnp.exp(sfx[0:1, :])
        u_t = lax.dot_general(v.astype(BF16), kw, (((0,), (0,)), ((), ())),
                              preferred_element_type=F32)
        state = state_ref[...] * decay + jnp.where(same_head, u_t, 0.0)
        state_ref[...] = state
        o = lax.dot_general(q.astype(BF16), state.astype(BF16), (((1,), (1,)), ((), ())),
                            preferred_element_type=F32)
        outs = []
        for hd in range(GLA_HEADS):
            oh = o[:, hd * GLA_DV:(hd + 1) * GLA_DV]
            ms = jnp.mean(oh * oh, axis=-1, keepdims=True)
            outs.append(oh * lax.rsqrt(ms + EPS))
        y = jnp.concatenate(outs, axis=1) * gn
        gate = r * (1.0 / (1.0 + jnp.exp(-r)))
        o_ref[0, rows, :] = (y * gate).astype(BF16)


def _gla(gla, loga, gn, tri):
    B, LP, _ = gla.shape
    nt = LP // TM
    return pl.pallas_call(
        _gla_kernel,
        grid=(B, nt),
        in_specs=[pl.BlockSpec((1, TM, _C_FQ), lambda b, i: (b, i, 0)),
                  pl.BlockSpec((1, TM, GLA_QK), lambda b, i: (b, i, 0)),
                  pl.BlockSpec((1, GLA_WIDTH), lambda b, i: (0, 0)),
                  pl.BlockSpec((CHUNK, CHUNK), lambda b, i: (0, 0))],
        out_specs=pl.BlockSpec((1, TM, GLA_WIDTH), lambda b, i: (b, i, 0)),
        out_shape=jax.ShapeDtypeStruct((B, LP, GLA_WIDTH), BF16),
        scratch_shapes=[pltpu.VMEM((GLA_WIDTH, GLA_QK), F32)],
        compiler_params=pltpu.CompilerParams(
            dimension_semantics=("parallel", "arbitrary"), vmem_limit_bytes=VMEM_LIMIT),
        name="gla",
    )(gla, loga, gn, tri)


def _fox_kernel(qa_ref, ka_ref, va_ref, gn_ref, o_ref):
    qi = pl.program_id(2)
    tq = qa_ref.shape[1]
    lane = lax.broadcasted_iota(jnp.int32, (tq, LANE), 1)
    low = lane < FOX_DH
    causal = (lax.broadcasted_iota(jnp.int32, (tq, tq), 1)
              <= lax.broadcasted_iota(jnp.int32, (tq, tq), 0))
    ys = []
    for hh in range(2):
        cols = slice(hh * LANE, (hh + 1) * LANE)
        q = qa_ref[0, :, cols]

        def scores(j):
            start = pl.multiple_of(j * tq, tq)
            k = ka_ref[0, pl.ds(start, tq), cols]
            v = va_ref[0, pl.ds(start, tq), cols]
            s = lax.dot_general(q, k, (((1,), (1,)), ((), ())), preferred_element_type=F32)
            return s, v

        def update(s, v, m, acc):
            m_new = jnp.maximum(m, jnp.max(s, axis=-1, keepdims=True))
            p = jnp.exp2(s - m_new)
            acc = jnp.exp2(m - m_new) * acc + _dot(p.astype(BF16), v)
            return m_new, acc

        def body(j, carry):
            s, v = scores(j)
            return update(s, v, *carry)

        m0 = jnp.full((tq, 1), NEG, F32)
        acc0 = jnp.zeros((tq, LANE), F32)
        m, acc = lax.fori_loop(0, qi, body, (m0, acc0))
        s, v = scores(qi)
        m, acc = update(jnp.where(causal, s, NEG), v, m, acc)

        o = acc * (1.0 / acc[:, FOX_DH:FOX_DH + 1])
        ms = jnp.sum(jnp.where(low, o * o, 0.0), axis=-1, keepdims=True) * (1.0 / FOX_DH)
        ys.append(o * lax.rsqrt(ms + EPS) * gn_ref[0, :, cols])
    o_ref[0] = jnp.where(low, ys[0], pltpu.roll(ys[1], FOX_DH, axis=1)).astype(BF16)


def _fox(qa, ka, va, gn):
    B, LP, _ = qa.shape
    nq = LP // TQ
    return pl.pallas_call(
        _fox_kernel,
        grid=(B, FOX_HEADS // 2, nq),
        in_specs=[pl.BlockSpec((1, TQ, 2 * LANE), lambda b, h, i: (b, i, h)),
                  pl.BlockSpec((1, LP, 2 * LANE), lambda b, h, i: (b, 0, h)),
                  pl.BlockSpec((1, LP, 2 * LANE), lambda b, h, i: (b, 0, h)),
                  pl.BlockSpec((1, 1, 2 * LANE), lambda b, h, i: (h, 0, 0))],
        out_specs=pl.BlockSpec((1, TQ, LANE), lambda b, h, i: (b, i, h)),
        out_shape=jax.ShapeDtypeStruct((B, LP, FOX_WIDTH), BF16),
        compiler_params=pltpu.CompilerParams(
            dimension_semantics=("parallel", "parallel", "arbitrary"), vmem_limit_bytes=VMEM_LIMIT),
        name="fox",
    )(qa, ka, va, gn)


def _ffn_kernel(h_ref, og_ref, of_ref, wog_ref, wof_ref, g_ref, wu_ref, wg_ref, cw_ref, cb_ref,
                wd_ref, fin_ref, out_ref, xn_ref, acc_ref, hbuf_ref, halo_ref, *, final):
    i = pl.program_id(1)
    tm = h_ref.shape[1]

    @pl.when(i == 0)
    def _():
        halo_ref[...] = jnp.zeros_like(halo_ref)

    h1 = h_ref[0] + (_dot(og_ref[0], wog_ref[...]) + _dot(of_ref[0], wof_ref[...]))
    acc_ref[...] = h1
    ms = jnp.mean(h1 * h1, axis=-1, keepdims=True)
    row = lax.broadcasted_iota(jnp.int32, (tm, 1), 0) + i * tm
    xn = jnp.where(row >= N_DUMMY, h1 * lax.rsqrt(ms + EPS) * g_ref[...], 0.0)
    xn_ref[...] = xn.astype(BF16)

    def conv(half, j, hcur):
        hbuf_ref[half, 0:SUBLANE, :] = halo_ref[half, j]
        hbuf_ref[half, SUBLANE:SUBLANE + tm, :] = hcur
        halo_ref[half, j] = hcur[tm - SUBLANE:tm, :]
        w = cw_ref[half, j]
        out = cb_ref[half, j][0:1, :] + hcur * w[2:3, :]
        out = out + hbuf_ref[half, SUBLANE - 1:SUBLANE - 1 + tm, :] * w[1:2, :]
        return out + hbuf_ref[half, SUBLANE - 2:SUBLANE - 2 + tm, :] * w[0:1, :]

    def body(j, carry):
        xnb = xn_ref[...]
        cu = conv(0, j, _dot(xnb, wu_ref[j]))
        cg = conv(1, j, _dot(xnb, wg_ref[j]))
        a = (cg * (1.0 / (1.0 + jnp.exp(-cg))) * cu).astype(BF16)
        acc_ref[...] += _dot(a, wd_ref[j])
        return carry

    lax.fori_loop(0, N_FF_CHUNKS, body, 0)
    y = acc_ref[...]
    if final:
        ms = jnp.mean(y * y, axis=-1, keepdims=True)
        y = y * lax.rsqrt(ms + EPS) * fin_ref[...]
    out_ref[0] = y


def _ffn(h, og, of, wog, wof, g, wu, wg, cw, cb, wd, fin, final):
    B, LP, D = h.shape
    nt = LP // TM
    resident = lambda shape: pl.BlockSpec(shape, lambda b, i: (0,) * len(shape),
                                          pipeline_mode=pl.Buffered(1))
    row_blk = lambda n: pl.BlockSpec((1, TM, n), lambda b, i: (b, i, 0))
    return pl.pallas_call(
        functools.partial(_ffn_kernel, final=final),
        grid=(B, nt),
        in_specs=[row_blk(D), row_blk(GLA_WIDTH), row_blk(FOX_WIDTH),
                  resident((GLA_WIDTH, D)), resident((FOX_WIDTH, D)), resident((1, D)),
                  resident((N_FF_CHUNKS, D, FF_CHUNK)), resident((N_FF_CHUNKS, D, FF_CHUNK)),
                  resident((2, N_FF_CHUNKS, SUBLANE, FF_CHUNK)),
                  resident((2, N_FF_CHUNKS, SUBLANE, FF_CHUNK)),
                  resident((N_FF_CHUNKS, FF_CHUNK, D)), resident((1, D))],
        out_specs=row_blk(D),
        out_shape=jax.ShapeDtypeStruct((B, LP, D), F32),
        scratch_shapes=[pltpu.VMEM((TM, D), BF16), pltpu.VMEM((TM, D), F32),
                        pltpu.VMEM((2, TM + SUBLANE, FF_CHUNK), F32),
                        pltpu.VMEM((2, N_FF_CHUNKS, SUBLANE, FF_CHUNK), F32)],
        compiler_params=pltpu.CompilerParams(
            dimension_semantics=("parallel", "arbitrary"), vmem_limit_bytes=VMEM_LIMIT),
        name="ffn",
    )(h, og, of, wog, wof, g, wu, wg, cw, cb, wd, fin)


def _selection_matrices():
    sq = np.zeros((LANE, FOX_HEADS * LANE), np.float32)
    sk = np.zeros((LANE, FOX_HEADS * LANE), np.float32)
    one = 3 * FOX_HEADS
    for h in range(FOX_HEADS):
        base = h * LANE + FOX_DH
        for p in range(3):
            sq[p * FOX_HEADS + h, base + p] = 1.0
            sk[one, base + p] = 1.0
            sq[one, base + 3 + p] = 1.0
            sk[p * FOX_HEADS + h, base + 3 + p] = -1.0
    return jnp.asarray(sq, BF16), jnp.asarray(sk, BF16)


def _chunk_cols(w):
    return jnp.transpose(w.reshape(w.shape[0], N_FF_CHUNKS, FF_CHUNK), (1, 0, 2))


def _pad_rows(w):
    w = jnp.pad(w, ((0, SUBLANE - w.shape[0]), (0, 0)))
    return _chunk_cols(w)


def kernel(x, meta_tokens, attn_norm, w_in, w_alpha_up, b_alpha, b_forget, gla_norm, fox_norm,
           w_out, ffn_norm, w_up, conv_w, conv_b, w_down, final_norm):
    B, S, D = x.shape
    depth = w_in.shape[0]
    assert D == D_MODEL and S % CHUNK == 0 and (FRONT + S) % TM == 0
    meta = jnp.broadcast_to(meta_tokens[None].astype(x.dtype), (B, N_META, D))
    h = jnp.concatenate([jnp.zeros((B, N_DUMMY, D), x.dtype), meta, x], axis=1)

    sq, sk = _selection_matrices()
    tri_lo = jnp.asarray(np.tril(np.ones((LANE, LANE), np.float32)), BF16)
    tri_up = jnp.asarray(np.triu(np.ones((CHUNK, CHUNK), np.float32)), BF16)
    c = np.cumsum((GLA_QK, GLA_QK, GLA_WIDTH, GLA_WIDTH, GLA_LOWRANK, FOX_WIDTH, FOX_WIDTH,
                   FOX_WIDTH, FOX_HEADS))

    for l in range(depth):
        w = w_in[l]
        w_main = jnp.concatenate(
            [w[:, :c[0]] * GLA_DK ** -0.5, w[:, c[0]:c[3]], w[:, c[4]:c[7]], w[:, c[7]:c[8]],
             w[:, c[3]:c[4]], jnp.zeros((D, LANE - FOX_HEADS - GLA_LOWRANK), w.dtype)],
            axis=1).astype(BF16)
        wa = jnp.zeros((LANE, GLA_QK), F32).at[FOX_HEADS:FOX_HEADS + GLA_LOWRANK].set(
            w_alpha_up[l]).astype(BF16)
        bf = jnp.zeros((1, LANE), F32).at[0, :FOX_HEADS].set(b_forget[l])
        gla, loga, qa, ka, va = _inproj(h, attn_norm[l][None], w_main, wa, b_alpha[l][None], bf,
                                        sq, sk, tri_lo)
        o_gla = _gla(gla, loga, gla_norm[l][None], tri_up)
        fgn = jnp.pad(fox_norm[l].reshape(FOX_HEADS, FOX_DH), ((0, 0), (0, LANE - FOX_DH)))
        o_fox = _fox(qa, ka, va, fgn.reshape(FOX_HEADS // 2, 1, 2 * LANE))
        wo = w_out[l].astype(BF16)
        cw = jnp.stack([_pad_rows(conv_w[l][:, :D_FF]), _pad_rows(conv_w[l][:, D_FF:])])
        cb = jnp.stack([_pad_rows(conv_b[l][None, :D_FF]), _pad_rows(conv_b[l][None, D_FF:])])
        h = _ffn(h, o_gla, o_fox, wo[:GLA_WIDTH], wo[GLA_WIDTH:], ffn_norm[l][None],
                 _chunk_cols(w_up[l][:, :D_FF]).astype(BF16),
                 _chunk_cols(w_up[l][:, D_FF:]).astype(BF16), cw, cb,
                 w_down[l].reshape(N_FF_CHUNKS, FF_CHUNK, D).astype(BF16), final_norm[None],
                 final=(l == depth - 1))
    return h[:, FRONT:]
```

```python
import functools

import jax
import jax.numpy as jnp
import numpy as np
from jax import lax
from jax.experimental import pallas as pl
from jax.experimental.pallas import tpu as pltpu

D_MODEL = 1024
N_META = 16
CHUNK = 64
EPS = 1e-6
GLA_HEADS = 4
GLA_DK = 64
GLA_DV = 128
GLA_LOWRANK = 16
GLA_TAU = 16.0
GLA_QK = GLA_HEADS * GLA_DK
GLA_WIDTH = GLA_HEADS * GLA_DV
FOX_HEADS = 8
FOX_DH = 64
FOX_WIDTH = FOX_HEADS * FOX_DH
D_FF = 2816
CONV_W = 3

LANE = 128
SUBLANE = 8
FRONT = 128
N_DUMMY = FRONT - N_META
TM = 384
TQ = 512
TK = 256
FOX_GROUP = 4
FF_CHUNK = 256
N_FF_CHUNKS = D_FF // FF_CHUNK
LOG2E = 1.4426950408889634
KEY_MASK = 16384.0
NEG = -1e30
VMEM_LIMIT = 56 * 1024 * 1024

_C_GLA = 0
_C_FQ = 1536
_C_FK = 2048
_C_FV = 2560
_C_SM = 3072
_N_MAIN = 3200

F32 = jnp.float32
BF16 = jnp.bfloat16


def _dot(a, b):
    return jnp.dot(a, b, preferred_element_type=F32)


def _log_sigmoid(x):
    return jnp.minimum(x, 0.0) - jnp.log1p(jnp.exp(-jnp.abs(x)))


def _split3(x):
    p1 = x.astype(BF16).astype(F32)
    r1 = x - p1
    p2 = r1.astype(BF16).astype(F32)
    p3 = (r1 - p2).astype(BF16).astype(F32)
    return p1, p2, p3


def _tri_sum(tri, x):
    p1, p2, p3 = _split3(x)
    return (_dot(tri, p1.astype(BF16)) + _dot(tri, p2.astype(BF16))) + _dot(tri, p3.astype(BF16))


def _inproj_kernel(h_ref, g_ref, w_ref, wa_ref, ba_ref, bf_ref, sq_ref, sk_ref, tri_ref,
                   gla_ref, loga_ref, qa_ref, ka_ref, va_ref, carry_ref):
    i = pl.program_id(1)
    tm = h_ref.shape[1]
    x = h_ref[0]
    ms = jnp.mean(x * x, axis=-1, keepdims=True)
    xn = (x * lax.rsqrt(ms + EPS) * g_ref[...]).astype(BF16)

    gla_ref[0] = _dot(xn, w_ref[:, _C_GLA:_C_FQ])
    fq = _dot(xn, w_ref[:, _C_FQ:_C_FK])
    fk = _dot(xn, w_ref[:, _C_FK:_C_FV])
    fv = _dot(xn, w_ref[:, _C_FV:_C_SM])
    sm = _dot(xn, w_ref[:, _C_SM:_N_MAIN])

    z = _dot(sm.astype(BF16), wa_ref[...]) + ba_ref[...]
    loga_ref[0] = _log_sigmoid(z) * (1.0 / GLA_TAU)

    lane = lax.broadcasted_iota(jnp.int32, (tm, LANE), 1)
    row = lax.broadcasted_iota(jnp.int32, (tm, LANE), 0) + i * tm
    real = row >= N_DUMMY
    head_lane = lane < FOX_HEADS
    lf = jnp.where(head_lane & real, _log_sigmoid(sm + bf_ref[...]), 0.0)

    @pl.when(i == 0)
    def _():
        carry_ref[...] = jnp.zeros_like(carry_ref)

    carry = carry_ref[0:1, :]
    tri = tri_ref[...]
    blocks = []
    for j in range(tm // LANE):
        cs = _tri_sum(tri, lf[j * LANE:(j + 1) * LANE]) + carry
        carry = cs[LANE - 1:LANE, :]
        blocks.append(cs)
    carry_ref[...] = jnp.broadcast_to(carry, carry_ref.shape)
    c = jnp.concatenate(blocks, axis=0) * LOG2E
    c = jnp.where(head_lane, jnp.where(real, c, KEY_MASK), 0.0)
    c1, c2, c3 = _split3(c)
    cc = (c1 + pltpu.roll(c2, FOX_HEADS, axis=1)) + (pltpu.roll(c3, 2 * FOX_HEADS, axis=1)
                                                      + jnp.where(lane == 3 * FOX_HEADS, 1.0, 0.0))
    cc = cc.astype(BF16)
    augq = _dot(cc, sq_ref[...])
    augk = _dot(cc, sk_ref[...])

    low = lane < FOX_DH
    ones_col = jnp.where(lane == FOX_DH, 1.0, 0.0)
    for h in range(FOX_HEADS):
        pair = slice((h // 2) * LANE, (h // 2 + 1) * LANE)
        blk = slice(h * LANE, (h + 1) * LANE)
        xq, xk, xv = fq[:, pair], fk[:, pair], fv[:, pair]
        if h % 2:
            xq = pltpu.roll(xq, FOX_DH, axis=1)
            xk = pltpu.roll(xk, FOX_DH, axis=1)
            xv = pltpu.roll(xv, FOX_DH, axis=1)
        qa_ref[0, :, blk] = jnp.where(low, xq * (LOG2E * FOX_DH ** -0.5), augq[:, blk]).astype(BF16)
        ka_ref[0, :, blk] = jnp.where(low, xk, augk[:, blk]).astype(BF16)
        va_ref[0, :, blk] = jnp.where(low, xv, ones_col).astype(BF16)


def _inproj(h, g, w_main, wa, ba, bf, sq, sk, tri):
    B, LP, D = h.shape
    nt = LP // TM
    const = lambda shape: pl.BlockSpec(shape, lambda b, i: (0,) * len(shape))
    row_blk = lambda n: pl.BlockSpec((1, TM, n), lambda b, i: (b, i, 0))
    return pl.pallas_call(
        _inproj_kernel,
        grid=(B, nt),
        in_specs=[row_blk(D), const((1, D)), const((D, _N_MAIN)), const((LANE, GLA_QK)),
                  const((1, GLA_QK)), const((1, LANE)), const((LANE, FOX_HEADS * LANE)),
                  const((LANE, FOX_HEADS * LANE)), const((LANE, LANE))],
        out_specs=[row_blk(_C_FQ), row_blk(GLA_QK), row_blk(FOX_HEADS * LANE),
                   row_blk(FOX_HEADS * LANE), row_blk(FOX_HEADS * LANE)],
        out_shape=[jax.ShapeDtypeStruct((B, LP, _C_FQ), F32),
                   jax.ShapeDtypeStruct((B, LP, GLA_QK), F32),
                   jax.ShapeDtypeStruct((B, LP, FOX_HEADS * LANE), BF16),
                   jax.ShapeDtypeStruct((B, LP, FOX_HEADS * LANE), BF16),
                   jax.ShapeDtypeStruct((B, LP, FOX_HEADS * LANE), BF16)],
        scratch_shapes=[pltpu.VMEM((SUBLANE, LANE), F32)],
        compiler_params=pltpu.CompilerParams(
            dimension_semantics=("parallel", "arbitrary"), vmem_limit_bytes=VMEM_LIMIT),
        name="inproj",
    )(h, g, w_main, wa, ba, bf, sq, sk, tri)


def _gla_kernel(gla_ref, loga_ref, gn_ref, tri_ref, o_ref, state_ref):
    i = pl.program_id(1)
    tm = gla_ref.shape[1]

    @pl.when(i == 0)
    def _():
        state_ref[...] = jnp.zeros_like(state_ref)

    tri = tri_ref[...]
    srow = lax.broadcasted_iota(jnp.int32, (GLA_WIDTH, GLA_QK), 0) // GLA_DV
    scol = lax.broadcasted_iota(jnp.int32, (GLA_WIDTH, GLA_QK), 1) // GLA_DK
    same_head = srow == scol
    gn = gn_ref[...]
    for s in range(tm // CHUNK):
        rows = slice(s * CHUNK, (s + 1) * CHUNK)
        row = lax.broadcasted_iota(jnp.int32, (CHUNK, 1), 0) + (i * tm + s * CHUNK)
        real = row >= N_DUMMY
        q = gla_ref[0, rows, 0:GLA_QK]
        k = jnp.where(real, gla_ref[0, rows, GLA_QK:2 * GLA_QK], 0.0)
        v = gla_ref[0, rows, 2 * GLA_QK:2 * GLA_QK + GLA_WIDTH]
        r = gla_ref[0, rows, 2 * GLA_QK + GLA_WIDTH:2 * GLA_QK + 2 * GLA_WIDTH]
        la = jnp.where(real, loga_ref[0, rows, :], 0.0)
        sfx = _tri_sum(tri, la)
        kw = (k * jnp.exp(sfx - la)).astype(BF16)
        decay = jnp.exp(sfx[0:1, :])
        u_t = lax.dot_general(v.astype(BF16), kw, (((0,), (0,)), ((), ())),
                              preferred_element_type=F32)
        state = state_ref[...] * decay + jnp.where(same_head, u_t, 0.0)
        state_ref[...] = state
        o = lax.dot_general(q.astype(BF16), state.astype(BF16), (((1,), (1,)), ((), ())),
                            preferred_element_type=F32)
        outs = []
        for hd in range(GLA_HEADS):
            oh = o[:, hd * GLA_DV:(hd + 1) * GLA_DV]
            ms = jnp.mean(oh * oh, axis=-1, keepdims=True)
            outs.append(oh * lax.rsqrt(ms + EPS))
        y = jnp.concatenate(outs, axis=1) * gn
        gate = r * (1.0 / (1.0 + jnp.exp(-r)))
        o_ref[0, rows, :] = (y * gate).astype(BF16)


def _gla(gla, loga, gn, tri):
    B, LP, _ = gla.shape
    nt = LP // TM
    return pl.pallas_call(
        _gla_kernel,
        grid=(B, nt),
        in_specs=[pl.BlockSpec((1, TM, _C_FQ), lambda b, i: (b, i, 0)),
                  pl.BlockSpec((1, TM, GLA_QK), lambda b, i: (b, i, 0)),
                  pl.BlockSpec((1, GLA_WIDTH), lambda b, i: (0, 0)),
                  pl.BlockSpec((CHUNK, CHUNK), lambda b, i: (0, 0))],
        out_specs=pl.BlockSpec((1, TM, GLA_WIDTH), lambda b, i: (b, i, 0)),
        out_shape=jax.ShapeDtypeStruct((B, LP, GLA_WIDTH), BF16),
        scratch_shapes=[pltpu.VMEM((GLA_WIDTH, GLA_QK), F32)],
        compiler_params=pltpu.CompilerParams(
            dimension_semantics=("parallel", "arbitrary"), vmem_limit_bytes=VMEM_LIMIT),
        name="gla",
    )(gla, loga, gn, tri)


def _fox_kernel(qa_ref, ka_ref, va_ref, gn_ref, o_ref, sa_ref, sb_ref, m_ref, acc_ref):
    t = pl.program_id(2)
    cols = [slice(hh * LANE, (hh + 1) * LANE) for hh in range(FOX_GROUP)]
    nt = (((1,), (1,)), ((), ()))

    def finish(rows, n):
        low = lax.broadcasted_iota(jnp.int32, (n, LANE), 1) < FOX_DH
        ys = []
        for hh in range(FOX_GROUP):
            acc = acc_ref[hh, 0:n, :]
            o = acc * (1.0 / acc[:, FOX_DH:FOX_DH + 1])
            ms = jnp.sum(jnp.where(low, o * o, 0.0), axis=-1, keepdims=True) * (1.0 / FOX_DH)
            ys.append(o * lax.rsqrt(ms + EPS) * gn_ref[0, :, cols[hh]])
        for pr in range(FOX_GROUP // 2):
            o_ref[0, rows, pr * LANE:(pr + 1) * LANE] = jnp.where(
                low, ys[2 * pr], pltpu.roll(ys[2 * pr + 1], FOX_DH, axis=1)).astype(BF16)

    def first_block(q_rows, n, causal):
        for hh in range(FOX_GROUP):
            q = qa_ref[0, q_rows, cols[hh]]
            s = lax.dot_general(q, ka_ref[0, 0:FRONT, cols[hh]], nt, preferred_element_type=F32)
            if causal:
                s = jnp.where(lax.broadcasted_iota(jnp.int32, (n, FRONT), 1)
                              <= lax.broadcasted_iota(jnp.int32, (n, FRONT), 0), s, NEG)
            m = jnp.broadcast_to(jnp.max(s, axis=-1, keepdims=True), (n, LANE))
            p = jnp.exp2(s - m)
            m_ref[hh, 0:n, :] = m
            acc_ref[hh, 0:n, :] = _dot(p.astype(BF16), va_ref[0, 0:FRONT, cols[hh]])

    @pl.when(t == 0)
    def _():
        first_block(slice(0, FRONT), FRONT, causal=True)
        finish(slice(0, FRONT), FRONT)

    @pl.when(t > 0)
    def _():
        r0 = pl.multiple_of(FRONT + (t - 1) * TQ, LANE)
        q_rows = pl.ds(r0, TQ)

        def key_rows(c):
            return pl.ds(pl.multiple_of(FRONT + c * TK, LANE), TK)

        def issue_scores(s_ref, c):
            for hh in range(FOX_GROUP):
                s_ref[hh] = lax.dot_general(qa_ref[0, q_rows, cols[hh]], ka_ref[0, key_rows(c), cols[hh]],
                                            nt, preferred_element_type=F32)

        def update(s_ref, c, diag_offset=None):
            if diag_offset is not None:
                visible = (lax.broadcasted_iota(jnp.int32, (TQ, TK), 1) + diag_offset
                           <= lax.broadcasted_iota(jnp.int32, (TQ, TK), 0))
            for hh in range(FOX_GROUP):
                s = s_ref[hh]
                if diag_offset is not None:
                    s = jnp.where(visible, s, NEG)
                m = m_ref[hh]
                m_new = jnp.maximum(m, jnp.broadcast_to(jnp.max(s, axis=-1, keepdims=True), (TQ, LANE)))
                p = jnp.exp2(s - jnp.concatenate([m_new] * (TK // LANE), axis=1))
                acc_ref[hh] = (jnp.exp2(m - m_new) * acc_ref[hh]
                               + _dot(p.astype(BF16), va_ref[0, key_rows(c), cols[hh]]))
                m_ref[hh] = m_new

        first_block(q_rows, TQ, causal=False)
        issue_scores(sa_ref, 0)

        def open_pair(i, carry):
            issue_scores(sb_ref, 2 * i + 1)
            update(sa_ref, 2 * i)
            issue_scores(sa_ref, 2 * i + 2)
            update(sb_ref, 2 * i + 1)
            return carry

        lax.fori_loop(0, t - 1, open_pair, 0)
        c0 = 2 * (t - 1)
        issue_scores(sb_ref, c0 + 1)
        update(sa_ref, c0, diag_offset=0)
        update(sb_ref, c0 + 1, diag_offset=TK)
        finish(q_rows, TQ)


def _fox(qa, ka, va, gn):
    B, LP, _ = qa.shape
    gw = FOX_GROUP * LANE
    assert TQ == 2 * TK and (LP - FRONT) % TQ == 0
    seq = lambda n: pl.BlockSpec((1, LP, n), lambda b, h, t: (b, 0, h))
    return pl.pallas_call(
        _fox_kernel,
        grid=(B, FOX_HEADS // FOX_GROUP, 1 + (LP - FRONT) // TQ),
        in_specs=[seq(gw), seq(gw), seq(gw), pl.BlockSpec((1, 1, gw), lambda b, h, t: (h, 0, 0))],
        out_specs=seq(gw // 2),
        out_shape=jax.ShapeDtypeStruct((B, LP, FOX_WIDTH), BF16),
        scratch_shapes=[pltpu.VMEM((FOX_GROUP, TQ, TK), F32), pltpu.VMEM((FOX_GROUP, TQ, TK), F32),
                        pltpu.VMEM((FOX_GROUP, TQ, LANE), F32), pltpu.VMEM((FOX_GROUP, TQ, LANE), F32)],
        compiler_params=pltpu.CompilerParams(
            dimension_semantics=("parallel", "parallel", "arbitrary"), vmem_limit_bytes=VMEM_LIMIT),
        name="fox",
    )(qa, ka, va, gn)


def _ffn_kernel(h_ref, og_ref, of_ref, wog_ref, wof_ref, g_ref, wu_ref, wg_ref, cw_ref, cb_ref,
                wd_ref, fin_ref, out_ref, xn_ref, acc_ref, hbuf_ref, halo_ref, *, final):
    i = pl.program_id(1)
    tm = h_ref.shape[1]

    @pl.when(i == 0)
    def _():
        halo_ref[...] = jnp.zeros_like(halo_ref)

    h1 = h_ref[0] + (_dot(og_ref[0], wog_ref[...]) + _dot(of_ref[0], wof_ref[...]))
    acc_ref[...] = h1
    ms = jnp.mean(h1 * h1, axis=-1, keepdims=True)
    row = lax.broadcasted_iota(jnp.int32, (tm, 1), 0) + i * tm
    xn = jnp.where(row >= N_DUMMY, h1 * lax.rsqrt(ms + EPS) * g_ref[...], 0.0)
    xn_ref[...] = xn.astype(BF16)

    def conv(half, j, hcur):
        hbuf_ref[half, 0:SUBLANE, :] = halo_ref[half, j]
        hbuf_ref[half, SUBLANE:SUBLANE + tm, :] = hcur
        halo_ref[half, j] = hcur[tm - SUBLANE:tm, :]
        w = cw_ref[half, j]
        out = cb_ref[half, j][0:1, :] + hcur * w[2:3, :]
        out = out + hbuf_ref[half, SUBLANE - 1:SUBLANE - 1 + tm, :] * w[1:2, :]
        return out + hbuf_ref[half, SUBLANE - 2:SUBLANE - 2 + tm, :] * w[0:1, :]

    def body(j, carry):
        xnb = xn_ref[...]
        cu = conv(0, j, _dot(xnb, wu_ref[j]))
        cg = conv(1, j, _dot(xnb, wg_ref[j]))
        a = (cg * (1.0 / (1.0 + jnp.exp(-cg))) * cu).astype(BF16)
        acc_ref[...] += _dot(a, wd_ref[j])
        return carry

    lax.fori_loop(0, N_FF_CHUNKS, body, 0)
    y = acc_ref[...]
    if final:
        ms = jnp.mean(y * y, axis=-1, keepdims=True)
        y = y * lax.rsqrt(ms + EPS) * fin_ref[...]
    out_ref[0] = y


def _ffn(h, og, of, wog, wof, g, wu, wg, cw, cb, wd, fin, final):
    B, LP, D = h.shape
    nt = LP // TM
    resident = lambda shape: pl.BlockSpec(shape, lambda b, i: (0,) * len(shape),
                                          pipeline_mode=pl.Buffered(1))
    row_blk = lambda n: pl.BlockSpec((1, TM, n), lambda b, i: (b, i, 0))
    return pl.pallas_call(
        functools.partial(_ffn_kernel, final=final),
        grid=(B, nt),
        in_specs=[row_blk(D), row_blk(GLA_WIDTH), row_blk(FOX_WIDTH),
                  resident((GLA_WIDTH, D)), resident((FOX_WIDTH, D)), resident((1, D)),
                  resident((N_FF_CHUNKS, D, FF_CHUNK)), resident((N_FF_CHUNKS, D, FF_CHUNK)),
                  resident((2, N_FF_CHUNKS, SUBLANE, FF_CHUNK)),
                  resident((2, N_FF_CHUNKS, SUBLANE, FF_CHUNK)),
                  resident((N_FF_CHUNKS, FF_CHUNK, D)), resident((1, D))],
        out_specs=row_blk(D),
        out_shape=jax.ShapeDtypeStruct((B, LP, D), F32),
        scratch_shapes=[pltpu.VMEM((TM, D), BF16), pltpu.VMEM((TM, D), F32),
                        pltpu.VMEM((2, TM + SUBLANE, FF_CHUNK), F32),
                        pltpu.VMEM((2, N_FF_CHUNKS, SUBLANE, FF_CHUNK), F32)],
        compiler_params=pltpu.CompilerParams(
            dimension_semantics=("parallel", "arbitrary"), vmem_limit_bytes=VMEM_LIMIT),
        name="ffn",
    )(h, og, of, wog, wof, g, wu, wg, cw, cb, wd, fin)


def _selection_matrices():
    sq = np.zeros((LANE, FOX_HEADS * LANE), np.float32)
    sk = np.zeros((LANE, FOX_HEADS * LANE), np.float32)
    one = 3 * FOX_HEADS
    for h in range(FOX_HEADS):
        base = h * LANE + FOX_DH
        for p in range(3):
            sq[p * FOX_HEADS + h, base + p] = 1.0
            sk[one, base + p] = 1.0
            sq[one, base + 3 + p] = 1.0
            sk[p * FOX_HEADS + h, base + 3 + p] = -1.0
    return jnp.asarray(sq, BF16), jnp.asarray(sk, BF16)


def _chunk_cols(w):
    return jnp.transpose(w.reshape(w.shape[0], N_FF_CHUNKS, FF_CHUNK), (1, 0, 2))


def _pad_rows(w):
    w = jnp.pad(w, ((0, SUBLANE - w.shape[0]), (0, 0)))
    return _chunk_cols(w)


def kernel(x, meta_tokens, attn_norm, w_in, w_alpha_up, b_alpha, b_forget, gla_norm, fox_norm,
           w_out, ffn_norm, w_up, conv_w, conv_b, w_down, final_norm):
    B, S, D = x.shape
    depth = w_in.shape[0]
    assert D == D_MODEL and S % CHUNK == 0 and (FRONT + S) % TM == 0
    meta = jnp.broadcast_to(meta_tokens[None].astype(x.dtype), (B, N_META, D))
    h = jnp.concatenate([jnp.zeros((B, N_DUMMY, D), x.dtype), meta, x], axis=1)

    sq, sk = _selection_matrices()
    tri_lo = jnp.asarray(np.tril(np.ones((LANE, LANE), np.float32)), BF16)
    tri_up = jnp.asarray(np.triu(np.ones((CHUNK, CHUNK), np.float32)), BF16)
    c = np.cumsum((GLA_QK, GLA_QK, GLA_WIDTH, GLA_WIDTH, GLA_LOWRANK, FOX_WIDTH, FOX_WIDTH,
                   FOX_WIDTH, FOX_HEADS))

    for l in range(depth):
        w = w_in[l]
        w_main = jnp.concatenate(
            [w[:, :c[0]] * GLA_DK ** -0.5, w[:, c[0]:c[3]], w[:, c[4]:c[7]], w[:, c[7]:c[8]],
             w[:, c[3]:c[4]], jnp.zeros((D, LANE - FOX_HEADS - GLA_LOWRANK), w.dtype)],
            axis=1).astype(BF16)
        wa = jnp.zeros((LANE, GLA_QK), F32).at[FOX_HEADS:FOX_HEADS + GLA_LOWRANK].set(
            w_alpha_up[l]).astype(BF16)
        bf = jnp.zeros((1, LANE), F32).at[0, :FOX_HEADS].set(b_forget[l])
        gla, loga, qa, ka, va = _inproj(h, attn_norm[l][None], w_main, wa, b_alpha[l][None], bf,
                                        sq, sk, tri_lo)
        o_gla = _gla(gla, loga, gla_norm[l][None], tri_up)
        fgn = jnp.pad(fox_norm[l].reshape(FOX_HEADS, FOX_DH), ((0, 0), (0, LANE - FOX_DH)))
        o_fox = _fox(qa, ka, va, fgn.reshape(FOX_HEADS // FOX_GROUP, 1, FOX_GROUP * LANE))
        wo = w_out[l].astype(BF16)
        cw = jnp.stack([_pad_rows(conv_w[l][:, :D_FF]), _pad_rows(conv_w[l][:, D_FF:])])
        cb = jnp.stack([_pad_rows(conv_b[l][None, :D_FF]), _pad_rows(conv_b[l][None, D_FF:])])
        h = _ffn(h, o_gla, o_fox, wo[:GLA_WIDTH], wo[GLA_WIDTH:], ffn_norm[l][None],
                 _chunk_cols(w_up[l][:, :D_FF]).astype(BF16),
                 _chunk_cols(w_up[l][:, D_FF:]).astype(BF16), cw, cb,
                 w_down[l].reshape(N_FF_CHUNKS, FF_CHUNK, D).astype(BF16), final_norm[None],
                 final=(l == depth - 1))
    return h[:, FRONT:]
```

```python
import functools

import jax
import jax.numpy as jnp
import numpy as np
from jax import lax
from jax.experimental import pallas as pl
from jax.experimental.pallas import tpu as pltpu

D_MODEL = 1024
N_META = 16
CHUNK = 64
EPS = 1e-6
GLA_HEADS = 4
GLA_DK = 64
GLA_DV = 128
GLA_LOWRANK = 16
GLA_TAU = 16.0
GLA_QK = GLA_HEADS * GLA_DK
GLA_WIDTH = GLA_HEADS * GLA_DV
FOX_HEADS = 8
FOX_DH = 64
FOX_WIDTH = FOX_HEADS * FOX_DH
D_FF = 2816
CONV_W = 3

LANE = 128
SUBLANE = 8
FRONT = 128
N_DUMMY = FRONT - N_META
TM = 384
TM_LARGE = 704
TQ = 512
TK = 256
FOX_GROUP = 4
FF_CHUNK = 256
N_FF_CHUNKS = D_FF // FF_CHUNK
LOG2E = 1.4426950408889634
KEY_MASK = 16384.0
NEG = -1e30
VMEM_LIMIT = 56 * 1024 * 1024

_C_GLA = 0
_C_FQ = 1536
_C_FK = 2048
_C_FV = 2560
_C_SM = 3072
_N_MAIN = 3200

F32 = jnp.float32
BF16 = jnp.bfloat16


def _dot(a, b):
    return jnp.dot(a, b, preferred_element_type=F32)


def _log_sigmoid(x):
    return jnp.minimum(x, 0.0) - jnp.log1p(jnp.exp(-jnp.abs(x)))


def _split3(x):
    p1 = x.astype(BF16).astype(F32)
    r1 = x - p1
    p2 = r1.astype(BF16).astype(F32)
    p3 = (r1 - p2).astype(BF16).astype(F32)
    return p1, p2, p3


def _tri_sum(tri, x):
    p1, p2, p3 = _split3(x)
    return (_dot(tri, p1.astype(BF16)) + _dot(tri, p2.astype(BF16))) + _dot(tri, p3.astype(BF16))


def _inproj_kernel(h_ref, g_ref, w_ref, wa_ref, ba_ref, bf_ref, sq_ref, sk_ref, tri_ref,
                   gla_ref, loga_ref, qa_ref, ka_ref, va_ref, carry_ref):
    i = pl.program_id(1)
    tm = h_ref.shape[1]
    x = h_ref[0]
    ms = jnp.mean(x * x, axis=-1, keepdims=True)
    xn = (x * lax.rsqrt(ms + EPS) * g_ref[...]).astype(BF16)

    gla_ref[0] = _dot(xn, w_ref[:, _C_GLA:_C_FQ])
    fq = _dot(xn, w_ref[:, _C_FQ:_C_FK])
    fk = _dot(xn, w_ref[:, _C_FK:_C_FV])
    fv = _dot(xn, w_ref[:, _C_FV:_C_SM])
    sm = _dot(xn, w_ref[:, _C_SM:_N_MAIN])

    z = _dot(sm.astype(BF16), wa_ref[...]) + ba_ref[...]
    loga_ref[0] = _log_sigmoid(z) * (1.0 / GLA_TAU)

    lane = lax.broadcasted_iota(jnp.int32, (tm, LANE), 1)
    row = lax.broadcasted_iota(jnp.int32, (tm, LANE), 0) + i * tm
    real = row >= N_DUMMY
    head_lane = lane < FOX_HEADS
    lf = jnp.where(head_lane & real, _log_sigmoid(sm + bf_ref[...]), 0.0)

    @pl.when(i == 0)
    def _():
        carry_ref[...] = jnp.zeros_like(carry_ref)

    carry = carry_ref[0:1, :]
    tri = tri_ref[...]
    blocks = []
    for j in range(tm // CHUNK):
        cs = _tri_sum(tri, lf[j * CHUNK:(j + 1) * CHUNK]) + carry
        carry = cs[CHUNK - 1:CHUNK, :]
        blocks.append(cs)
    carry_ref[...] = jnp.broadcast_to(carry, carry_ref.shape)
    c = jnp.concatenate(blocks, axis=0) * LOG2E
    c = jnp.where(head_lane, jnp.where(real, c, KEY_MASK), 0.0)
    c1, c2, c3 = _split3(c)
    cc = (c1 + pltpu.roll(c2, FOX_HEADS, axis=1)) + (pltpu.roll(c3, 2 * FOX_HEADS, axis=1)
                                                      + jnp.where(lane == 3 * FOX_HEADS, 1.0, 0.0))
    cc = cc.astype(BF16)
    augq = _dot(cc, sq_ref[...])
    augk = _dot(cc, sk_ref[...])

    low = lane < FOX_DH
    ones_col = jnp.where(lane == FOX_DH, 1.0, 0.0)
    for h in range(FOX_HEADS):
        pair = slice((h // 2) * LANE, (h // 2 + 1) * LANE)
        blk = slice(h * LANE, (h + 1) * LANE)
        xq, xk, xv = fq[:, pair], fk[:, pair], fv[:, pair]
        if h % 2:
            xq = pltpu.roll(xq, FOX_DH, axis=1)
            xk = pltpu.roll(xk, FOX_DH, axis=1)
            xv = pltpu.roll(xv, FOX_DH, axis=1)
        qa_ref[0, :, blk] = jnp.where(low, xq * (LOG2E * FOX_DH ** -0.5), augq[:, blk]).astype(BF16)
        ka_ref[0, :, blk] = jnp.where(low, xk, augk[:, blk]).astype(BF16)
        va_ref[0, :, blk] = jnp.where(low, xv, ones_col).astype(BF16)


def _row_tile(lp):
    return TM_LARGE if lp % TM_LARGE == 0 else TM


def _inproj(h, g, w_main, wa, ba, bf, sq, sk, tri):
    B, LP, D = h.shape
    tm = _row_tile(LP)
    nt = LP // tm
    const = lambda shape: pl.BlockSpec(shape, lambda b, i: (0,) * len(shape),
                                       pipeline_mode=pl.Buffered(1))
    row_blk = lambda n: pl.BlockSpec((1, tm, n), lambda b, i: (b, i, 0))
    return pl.pallas_call(
        _inproj_kernel,
        grid=(B, nt),
        in_specs=[row_blk(D), const((1, D)), const((D, _N_MAIN)), const((LANE, GLA_QK)),
                  const((1, GLA_QK)), const((1, LANE)), const((LANE, FOX_HEADS * LANE)),
                  const((LANE, FOX_HEADS * LANE)), const((CHUNK, CHUNK))],
        out_specs=[row_blk(_C_FQ), row_blk(GLA_QK), row_blk(FOX_HEADS * LANE),
                   row_blk(FOX_HEADS * LANE), row_blk(FOX_HEADS * LANE)],
        out_shape=[jax.ShapeDtypeStruct((B, LP, _C_FQ), F32),
                   jax.ShapeDtypeStruct((B, LP, GLA_QK), F32),
                   jax.ShapeDtypeStruct((B, LP, FOX_HEADS * LANE), BF16),
                   jax.ShapeDtypeStruct((B, LP, FOX_HEADS * LANE), BF16),
                   jax.ShapeDtypeStruct((B, LP, FOX_HEADS * LANE), BF16)],
        scratch_shapes=[pltpu.VMEM((SUBLANE, LANE), F32)],
        compiler_params=pltpu.CompilerParams(
            dimension_semantics=("parallel", "arbitrary"), vmem_limit_bytes=VMEM_LIMIT),
        name="inproj",
    )(h, g, w_main, wa, ba, bf, sq, sk, tri)


def _gla_kernel(gla_ref, loga_ref, gn_ref, tri_ref, o_ref, state_ref):
    i = pl.program_id(1)
    tm = gla_ref.shape[1]

    @pl.when(i == 0)
    def _():
        state_ref[...] = jnp.zeros_like(state_ref)

    tri = tri_ref[...]
    srow = lax.broadcasted_iota(jnp.int32, (GLA_WIDTH, GLA_QK), 0) // GLA_DV
    scol = lax.broadcasted_iota(jnp.int32, (GLA_WIDTH, GLA_QK), 1) // GLA_DK
    same_head = srow == scol
    gn = gn_ref[...]
    for s in range(tm // CHUNK):
        rows = slice(s * CHUNK, (s + 1) * CHUNK)
        row = lax.broadcasted_iota(jnp.int32, (CHUNK, 1), 0) + (i * tm + s * CHUNK)
        real = row >= N_DUMMY
        q = gla_ref[0, rows, 0:GLA_QK]
        k = jnp.where(real, gla_ref[0, rows, GLA_QK:2 * GLA_QK], 0.0)
        v = gla_ref[0, rows, 2 * GLA_QK:2 * GLA_QK + GLA_WIDTH]
        r = gla_ref[0, rows, 2 * GLA_QK + GLA_WIDTH:2 * GLA_QK + 2 * GLA_WIDTH]
        la = jnp.where(real, loga_ref[0, rows, :], 0.0)
        sfx = _tri_sum(tri, la)
        kw = (k * jnp.exp(sfx - la)).astype(BF16)
        decay = jnp.exp(sfx[0:1, :])
        u_t = lax.dot_general(v.astype(BF16), kw, (((0,), (0,)), ((), ())),
                              preferred_element_type=F32)
        state = state_ref[...] * decay + jnp.where(same_head, u_t, 0.0)
        state_ref[...] = state
        o = lax.dot_general(q.astype(BF16), state.astype(BF16), (((1,), (1,)), ((), ())),
                            preferred_element_type=F32)
        outs = []
        for hd in range(GLA_HEADS):
            oh = o[:, hd * GLA_DV:(hd + 1) * GLA_DV]
            ms = jnp.mean(oh * oh, axis=-1, keepdims=True)
            outs.append(oh * lax.rsqrt(ms + EPS))
        y = jnp.concatenate(outs, axis=1) * gn
        gate = r * (1.0 / (1.0 + jnp.exp(-r)))
        o_ref[0, rows, :] = (y * gate).astype(BF16)


def _gla(gla, loga, gn, tri):
    B, LP, _ = gla.shape
    nt = LP // TM
    return pl.pallas_call(
        _gla_kernel,
        grid=(B, nt),
        in_specs=[pl.BlockSpec((1, TM, _C_FQ), lambda b, i: (b, i, 0)),
                  pl.BlockSpec((1, TM, GLA_QK), lambda b, i: (b, i, 0)),
                  pl.BlockSpec((1, GLA_WIDTH), lambda b, i: (0, 0)),
                  pl.BlockSpec((CHUNK, CHUNK), lambda b, i: (0, 0))],
        out_specs=pl.BlockSpec((1, TM, GLA_WIDTH), lambda b, i: (b, i, 0)),
        out_shape=jax.ShapeDtypeStruct((B, LP, GLA_WIDTH), BF16),
        scratch_shapes=[pltpu.VMEM((GLA_WIDTH, GLA_QK), F32)],
        compiler_params=pltpu.CompilerParams(
            dimension_semantics=("parallel", "arbitrary"), vmem_limit_bytes=VMEM_LIMIT),
        name="gla",
    )(gla, loga, gn, tri)


def _fox_kernel(qa_ref, ka_ref, va_ref, gn_ref, o_ref, sa_ref, sb_ref, m_ref, acc_ref):
    t = pl.program_id(2)
    cols = [slice(hh * LANE, (hh + 1) * LANE) for hh in range(FOX_GROUP)]
    nt = (((1,), (1,)), ((), ()))

    def finish(rows, n):
        low = lax.broadcasted_iota(jnp.int32, (n, LANE), 1) < FOX_DH
        ys = []
        for hh in range(FOX_GROUP):
            acc = acc_ref[hh, 0:n, :]
            o = acc * (1.0 / acc[:, FOX_DH:FOX_DH + 1])
            ms = jnp.sum(jnp.where(low, o * o, 0.0), axis=-1, keepdims=True) * (1.0 / FOX_DH)
            ys.append(o * lax.rsqrt(ms + EPS) * gn_ref[0, :, cols[hh]])
        for pr in range(FOX_GROUP // 2):
            o_ref[0, rows, pr * LANE:(pr + 1) * LANE] = jnp.where(
                low, ys[2 * pr], pltpu.roll(ys[2 * pr + 1], FOX_DH, axis=1)).astype(BF16)

    def first_block(q_rows, n, causal):
        for hh in range(FOX_GROUP):
            q = qa_ref[0, q_rows, cols[hh]]
            s = lax.dot_general(q, ka_ref[0, 0:FRONT, cols[hh]], nt, preferred_element_type=F32)
            if causal:
                s = jnp.where(lax.broadcasted_iota(jnp.int32, (n, FRONT), 1)
                              <= lax.broadcasted_iota(jnp.int32, (n, FRONT), 0), s, NEG)
            m = jnp.broadcast_to(jnp.max(s, axis=-1, keepdims=True), (n, LANE))
            p = jnp.exp2(s - m)
            m_ref[hh, 0:n, :] = m
            acc_ref[hh, 0:n, :] = _dot(p.astype(BF16), va_ref[0, 0:FRONT, cols[hh]])

    @pl.when(t == 0)
    def _():
        first_block(slice(0, FRONT), FRONT, causal=True)
        finish(slice(0, FRONT), FRONT)

    @pl.when(t > 0)
    def _():
        r0 = pl.multiple_of(FRONT + (t - 1) * TQ, LANE)
        q_rows = pl.ds(r0, TQ)

        def key_rows(c):
            return pl.ds(pl.multiple_of(FRONT + c * TK, LANE), TK)

        def issue_scores(s_ref, c):
            for hh in range(FOX_GROUP):
                s_ref[hh] = lax.dot_general(qa_ref[0, q_rows, cols[hh]], ka_ref[0, key_rows(c), cols[hh]],
                                            nt, preferred_element_type=F32)

        def update(s_ref, c, diag_offset=None):
            if diag_offset is not None:
                visible = (lax.broadcasted_iota(jnp.int32, (TQ, TK), 1) + diag_offset
                           <= lax.broadcasted_iota(jnp.int32, (TQ, TK), 0))
            for hh in range(FOX_GROUP):
                s = s_ref[hh]
                if diag_offset is not None:
                    s = jnp.where(visible, s, NEG)
                m = m_ref[hh]
                m_new = jnp.maximum(m, jnp.broadcast_to(jnp.max(s, axis=-1, keepdims=True), (TQ, LANE)))
                p = jnp.exp2(s - jnp.concatenate([m_new] * (TK // LANE), axis=1))
                acc_ref[hh] = (jnp.exp2(m - m_new) * acc_ref[hh]
                               + _dot(p.astype(BF16), va_ref[0, key_rows(c), cols[hh]]))
                m_ref[hh] = m_new

        first_block(q_rows, TQ, causal=False)
        issue_scores(sa_ref, 0)

        def open_pair(i, carry):
            issue_scores(sb_ref, 2 * i + 1)
            update(sa_ref, 2 * i)
            issue_scores(sa_ref, 2 * i + 2)
            update(sb_ref, 2 * i + 1)
            return carry

        lax.fori_loop(0, t - 1, open_pair, 0)
        c0 = 2 * (t - 1)
        issue_scores(sb_ref, c0 + 1)
        update(sa_ref, c0, diag_offset=0)
        update(sb_ref, c0 + 1, diag_offset=TK)
        finish(q_rows, TQ)


def _fox(qa, ka, va, gn):
    B, LP, _ = qa.shape
    gw = FOX_GROUP * LANE
    assert TQ == 2 * TK and (LP - FRONT) % TQ == 0
    seq = lambda n: pl.BlockSpec((1, LP, n), lambda b, h, t: (b, 0, h))
    return pl.pallas_call(
        _fox_kernel,
        grid=(B, FOX_HEADS // FOX_GROUP, 1 + (LP - FRONT) // TQ),
        in_specs=[seq(gw), seq(gw), seq(gw), pl.BlockSpec((1, 1, gw), lambda b, h, t: (h, 0, 0))],
        out_specs=seq(gw // 2),
        out_shape=jax.ShapeDtypeStruct((B, LP, FOX_WIDTH), BF16),
        scratch_shapes=[pltpu.VMEM((FOX_GROUP, TQ, TK), F32), pltpu.VMEM((FOX_GROUP, TQ, TK), F32),
                        pltpu.VMEM((FOX_GROUP, TQ, LANE), F32), pltpu.VMEM((FOX_GROUP, TQ, LANE), F32)],
        compiler_params=pltpu.CompilerParams(
            dimension_semantics=("parallel", "parallel", "arbitrary"), vmem_limit_bytes=VMEM_LIMIT),
        name="fox",
    )(qa, ka, va, gn)


def _ffn_kernel(h_ref, og_ref, of_ref, wog_ref, wof_ref, g_ref, wu_ref, wg_ref, cw_ref, cb_ref,
                wd_ref, fin_ref, out_ref, xn_ref, acc_ref, ha_ref, hb_ref, halo_ref, *, final):
    i = pl.program_id(1)
    tm = h_ref.shape[1]

    @pl.when(i == 0)
    def _():
        halo_ref[...] = jnp.zeros_like(halo_ref)

    h1 = h_ref[0] + (_dot(og_ref[0], wog_ref[...]) + _dot(of_ref[0], wof_ref[...]))
    acc_ref[...] = h1
    ms = jnp.mean(h1 * h1, axis=-1, keepdims=True)
    row = lax.broadcasted_iota(jnp.int32, (tm, 1), 0) + i * tm
    xn = jnp.where(row >= N_DUMMY, h1 * lax.rsqrt(ms + EPS) * g_ref[...], 0.0)
    xn_ref[...] = xn.astype(BF16)

    def up(buf_ref, j):
        xnb = xn_ref[...]
        buf_ref[0, SUBLANE:SUBLANE + tm, :] = _dot(xnb, wu_ref[j])
        buf_ref[1, SUBLANE:SUBLANE + tm, :] = _dot(xnb, wg_ref[j])

    def conv(buf_ref, half, j):
        buf_ref[half, 0:SUBLANE, :] = halo_ref[half, j]
        halo_ref[half, j] = buf_ref[half, tm:tm + SUBLANE, :]
        w = cw_ref[half, j]
        out = cb_ref[half, j][0:1, :] + buf_ref[half, SUBLANE:SUBLANE + tm, :] * w[2:3, :]
        out = out + buf_ref[half, SUBLANE - 1:SUBLANE - 1 + tm, :] * w[1:2, :]
        return out + buf_ref[half, SUBLANE - 2:SUBLANE - 2 + tm, :] * w[0:1, :]

    def down(buf_ref, j):
        cu = conv(buf_ref, 0, j)
        cg = conv(buf_ref, 1, j)
        a = (cg * (1.0 / (1.0 + jnp.exp(-cg))) * cu).astype(BF16)
        acc_ref[...] += _dot(a, wd_ref[j])

    def pair(p, carry):
        up(hb_ref, 2 * p + 1)
        down(ha_ref, 2 * p)
        up(ha_ref, 2 * p + 2)
        down(hb_ref, 2 * p + 1)
        return carry

    assert N_FF_CHUNKS % 2 == 1
    up(ha_ref, 0)
    lax.fori_loop(0, N_FF_CHUNKS // 2, pair, 0)
    down(ha_ref, N_FF_CHUNKS - 1)
    y = acc_ref[...]
    if final:
        ms = jnp.mean(y * y, axis=-1, keepdims=True)
        y = y * lax.rsqrt(ms + EPS) * fin_ref[...]
    out_ref[0] = y


def _ffn(h, og, of, wog, wof, g, wu, wg, cw, cb, wd, fin, final):
    B, LP, D = h.shape
    tm = _row_tile(LP)
    nt = LP // tm
    resident = lambda shape: pl.BlockSpec(shape, lambda b, i: (0,) * len(shape),
                                          pipeline_mode=pl.Buffered(1))
    row_blk = lambda n: pl.BlockSpec((1, tm, n), lambda b, i: (b, i, 0))
    return pl.pallas_call(
        functools.partial(_ffn_kernel, final=final),
        grid=(B, nt),
        in_specs=[row_blk(D), row_blk(GLA_WIDTH), row_blk(FOX_WIDTH),
                  resident((GLA_WIDTH, D)), resident((FOX_WIDTH, D)), resident((1, D)),
                  resident((N_FF_CHUNKS, D, FF_CHUNK)), resident((N_FF_CHUNKS, D, FF_CHUNK)),
                  resident((2, N_FF_CHUNKS, SUBLANE, FF_CHUNK)),
                  resident((2, N_FF_CHUNKS, SUBLANE, FF_CHUNK)),
                  resident((N_FF_CHUNKS, FF_CHUNK, D)), resident((1, D))],
        out_specs=row_blk(D),
        out_shape=jax.ShapeDtypeStruct((B, LP, D), F32),
        scratch_shapes=[pltpu.VMEM((tm, D), BF16), pltpu.VMEM((tm, D), F32),
                        pltpu.VMEM((2, tm + SUBLANE, FF_CHUNK), F32),
                        pltpu.VMEM((2, tm + SUBLANE, FF_CHUNK), F32),
                        pltpu.VMEM((2, N_FF_CHUNKS, SUBLANE, FF_CHUNK), F32)],
        compiler_params=pltpu.CompilerParams(
            dimension_semantics=("parallel", "arbitrary"), vmem_limit_bytes=VMEM_LIMIT),
        name="ffn",
    )(h, og, of, wog, wof, g, wu, wg, cw, cb, wd, fin)


def _selection_matrices():
    sq = np.zeros((LANE, FOX_HEADS * LANE), np.float32)
    sk = np.zeros((LANE, FOX_HEADS * LANE), np.float32)
    one = 3 * FOX_HEADS
    for h in range(FOX_HEADS):
        base = h * LANE + FOX_DH
        for p in range(3):
            sq[p * FOX_HEADS + h, base + p] = 1.0
            sk[one, base + p] = 1.0
            sq[one, base + 3 + p] = 1.0
            sk[p * FOX_HEADS + h, base + 3 + p] = -1.0
    return jnp.asarray(sq, BF16), jnp.asarray(sk, BF16)


def _chunk_cols(w):
    return jnp.transpose(w.reshape(w.shape[0], N_FF_CHUNKS, FF_CHUNK), (1, 0, 2))


def _pad_rows(w):
    w = jnp.pad(w, ((0, SUBLANE - w.shape[0]), (0, 0)))
    return _chunk_cols(w)


def kernel(x, meta_tokens, attn_norm, w_in, w_alpha_up, b_alpha, b_forget, gla_norm, fox_norm,
           w_out, ffn_norm, w_up, conv_w, conv_b, w_down, final_norm):
    B, S, D = x.shape
    depth = w_in.shape[0]
    assert D == D_MODEL and S % CHUNK == 0 and (FRONT + S) % TM == 0
    meta = jnp.broadcast_to(meta_tokens[None].astype(x.dtype), (B, N_META, D))
    h = jnp.concatenate([jnp.zeros((B, N_DUMMY, D), x.dtype), meta, x], axis=1)

    sq, sk = _selection_matrices()
    tri_lo = jnp.asarray(np.tril(np.ones((CHUNK, CHUNK), np.float32)), BF16)
    tri_up = jnp.asarray(np.triu(np.ones((CHUNK, CHUNK), np.float32)), BF16)
    c = np.cumsum((GLA_QK, GLA_QK, GLA_WIDTH, GLA_WIDTH, GLA_LOWRANK, FOX_WIDTH, FOX_WIDTH,
                   FOX_WIDTH, FOX_HEADS))

    for l in range(depth):
        w = w_in[l]
        w_main = jnp.concatenate(
            [w[:, :c[0]] * GLA_DK ** -0.5, w[:, c[0]:c[3]], w[:, c[4]:c[7]], w[:, c[7]:c[8]],
             w[:, c[3]:c[4]], jnp.zeros((D, LANE - FOX_HEADS - GLA_LOWRANK), w.dtype)],
            axis=1).astype(BF16)
        wa = jnp.zeros((LANE, GLA_QK), F32).at[FOX_HEADS:FOX_HEADS + GLA_LOWRANK].set(
            w_alpha_up[l]).astype(BF16)
        bf = jnp.zeros((1, LANE), F32).at[0, :FOX_HEADS].set(b_forget[l])
        gla, loga, qa, ka, va = _inproj(h, attn_norm[l][None], w_main, wa, b_alpha[l][None], bf,
                                        sq, sk, tri_lo)
        o_gla = _gla(gla, loga, gla_norm[l][None], tri_up)
        fgn = jnp.pad(fox_norm[l].reshape(FOX_HEADS, FOX_DH), ((0, 0), (0, LANE - FOX_DH)))
        o_fox = _fox(qa, ka, va, fgn.reshape(FOX_HEADS // FOX_GROUP, 1, FOX_GROUP * LANE))
        wo = w_out[l].astype(BF16)
        cw = jnp.stack([_pad_rows(conv_w[l][:, :D_FF]), _pad_rows(conv_w[l][:, D_FF:])])
        cb = jnp.stack([_pad_rows(conv_b[l][None, :D_FF]), _pad_rows(conv_b[l][None, D_FF:])])
        h = _ffn(h, o_gla, o_fox, wo[:GLA_WIDTH], wo[GLA_WIDTH:], ffn_norm[l][None],
                 _chunk_cols(w_up[l][:, :D_FF]).astype(BF16),
                 _chunk_cols(w_up[l][:, D_FF:]).astype(BF16), cw, cb,
                 w_down[l].reshape(N_FF_CHUNKS, FF_CHUNK, D).astype(BF16), final_norm[None],
                 final=(l == depth - 1))
    return h[:, FRONT:]
```

```python
import functools

import jax
import jax.numpy as jnp
import numpy as np
from jax import lax
from jax.experimental import pallas as pl
from jax.experimental.pallas import tpu as pltpu

D_MODEL = 1024
N_META = 16
CHUNK = 64
EPS = 1e-6
GLA_HEADS = 4
GLA_DK = 64
GLA_DV = 128
GLA_LOWRANK = 16
GLA_TAU = 16.0
GLA_QK = GLA_HEADS * GLA_DK
GLA_WIDTH = GLA_HEADS * GLA_DV
FOX_HEADS = 8
FOX_DH = 64
FOX_WIDTH = FOX_HEADS * FOX_DH
D_FF = 2816
CONV_W = 3

LANE = 128
SUBLANE = 8
FRONT = 128
N_DUMMY = FRONT - N_META
ROW_TILES = (704, 384, 128)
GLA_TILES = (384, 128)
TQ = 1024
TK = 512
FOX_GROUP = 2
FF_CHUNK = 256
N_FF_CHUNKS = D_FF // FF_CHUNK
LOG2E = 1.4426950408889634
KEY_MASK = 16384.0
NEG = -1e30
VMEM_LIMIT = 56 * 1024 * 1024

_C_GLA = 0
_C_FQ = 1536
_C_FK = 2048
_C_FV = 2560
_C_SM = 3072
_N_MAIN = 3200

F32 = jnp.float32
BF16 = jnp.bfloat16


def _dot(a, b):
    return jnp.dot(a, b, preferred_element_type=F32)


def _log_sigmoid(x):
    return jnp.minimum(x, 0.0) - jnp.log1p(jnp.exp(-jnp.abs(x)))


def _split3(x):
    p1 = x.astype(BF16).astype(F32)
    r1 = x - p1
    p2 = r1.astype(BF16).astype(F32)
    p3 = (r1 - p2).astype(BF16).astype(F32)
    return p1, p2, p3


def _tri_sum(tri, x):
    p1, p2, p3 = _split3(x)
    return (_dot(tri, p1.astype(BF16)) + _dot(tri, p2.astype(BF16))) + _dot(tri, p3.astype(BF16))


def _inproj_kernel(h_ref, g_ref, w_ref, wa_ref, ba_ref, bf_ref, sq_ref, sk_ref, tri_ref,
                   gla_ref, loga_ref, qa_ref, ka_ref, va_ref, carry_ref):
    i = pl.program_id(1)
    tm = h_ref.shape[1]
    x = h_ref[0]
    ms = jnp.mean(x * x, axis=-1, keepdims=True)
    xn = (x * lax.rsqrt(ms + EPS) * g_ref[...]).astype(BF16)

    gla_ref[0] = _dot(xn, w_ref[:, _C_GLA:_C_FQ])
    fq = _dot(xn, w_ref[:, _C_FQ:_C_FK])
    fk = _dot(xn, w_ref[:, _C_FK:_C_FV])
    fv = _dot(xn, w_ref[:, _C_FV:_C_SM])
    sm = _dot(xn, w_ref[:, _C_SM:_N_MAIN])

    z = _dot(sm.astype(BF16), wa_ref[...]) + ba_ref[...]
    loga_ref[0] = _log_sigmoid(z) * (1.0 / GLA_TAU)

    lane = lax.broadcasted_iota(jnp.int32, (tm, LANE), 1)
    row = lax.broadcasted_iota(jnp.int32, (tm, LANE), 0) + i * tm
    real = row >= N_DUMMY
    head_lane = lane < FOX_HEADS
    lf = jnp.where(head_lane & real, _log_sigmoid(sm + bf_ref[...]), 0.0)

    @pl.when(i == 0)
    def _():
        carry_ref[...] = jnp.zeros_like(carry_ref)

    carry = carry_ref[0:1, :]
    tri = tri_ref[...]
    blocks = []
    for j in range(tm // CHUNK):
        cs = _tri_sum(tri, lf[j * CHUNK:(j + 1) * CHUNK]) + carry
        carry = cs[CHUNK - 1:CHUNK, :]
        blocks.append(cs)
    carry_ref[...] = jnp.broadcast_to(carry, carry_ref.shape)
    c = jnp.concatenate(blocks, axis=0) * LOG2E
    c = jnp.where(head_lane, jnp.where(real, c, KEY_MASK), 0.0)
    c1, c2, c3 = _split3(c)
    cc = (c1 + pltpu.roll(c2, FOX_HEADS, axis=1)) + (pltpu.roll(c3, 2 * FOX_HEADS, axis=1)
                                                      + jnp.where(lane == 3 * FOX_HEADS, 1.0, 0.0))
    cc = cc.astype(BF16)
    augq = _dot(cc, sq_ref[...])
    augk = _dot(cc, sk_ref[...])

    low = lane < FOX_DH
    ones_col = jnp.where(lane == FOX_DH, 1.0, 0.0)
    for h in range(FOX_HEADS):
        pair = slice((h // 2) * LANE, (h // 2 + 1) * LANE)
        blk = slice(h * LANE, (h + 1) * LANE)
        xq, xk, xv = fq[:, pair], fk[:, pair], fv[:, pair]
        if h % 2:
            xq = pltpu.roll(xq, FOX_DH, axis=1)
            xk = pltpu.roll(xk, FOX_DH, axis=1)
            xv = pltpu.roll(xv, FOX_DH, axis=1)
        qa_ref[0, :, blk] = jnp.where(low, xq * (LOG2E * FOX_DH ** -0.5), augq[:, blk]).astype(BF16)
        ka_ref[0, :, blk] = jnp.where(low, xk, augk[:, blk]).astype(BF16)
        va_ref[0, :, blk] = jnp.where(low, xv, ones_col).astype(BF16)


def _row_tile(lp, candidates=ROW_TILES):
    return next(tm for tm in candidates if lp % tm == 0)


def _inproj(h, g, w_main, wa, ba, bf, sq, sk, tri):
    B, LP, D = h.shape
    tm = _row_tile(LP)
    nt = LP // tm
    const = lambda shape: pl.BlockSpec(shape, lambda b, i: (0,) * len(shape),
                                       pipeline_mode=pl.Buffered(1))
    row_blk = lambda n: pl.BlockSpec((1, tm, n), lambda b, i: (b, i, 0))
    return pl.pallas_call(
        _inproj_kernel,
        grid=(B, nt),
        in_specs=[row_blk(D), const((1, D)), const((D, _N_MAIN)), const((LANE, GLA_QK)),
                  const((1, GLA_QK)), const((1, LANE)), const((LANE, FOX_HEADS * LANE)),
                  const((LANE, FOX_HEADS * LANE)), const((CHUNK, CHUNK))],
        out_specs=[row_blk(_C_FQ), row_blk(GLA_QK), row_blk(FOX_HEADS * LANE),
                   row_blk(FOX_HEADS * LANE), row_blk(FOX_HEADS * LANE)],
        out_shape=[jax.ShapeDtypeStruct((B, LP, _C_FQ), F32),
                   jax.ShapeDtypeStruct((B, LP, GLA_QK), F32),
                   jax.ShapeDtypeStruct((B, LP, FOX_HEADS * LANE), BF16),
                   jax.ShapeDtypeStruct((B, LP, FOX_HEADS * LANE), BF16),
                   jax.ShapeDtypeStruct((B, LP, FOX_HEADS * LANE), BF16)],
        scratch_shapes=[pltpu.VMEM((SUBLANE, LANE), F32)],
        compiler_params=pltpu.CompilerParams(
            dimension_semantics=("parallel", "arbitrary"), vmem_limit_bytes=VMEM_LIMIT),
        name="inproj",
    )(h, g, w_main, wa, ba, bf, sq, sk, tri)


def _gla_kernel(gla_ref, loga_ref, gn_ref, tri_ref, o_ref, state_ref):
    i = pl.program_id(1)
    tm = gla_ref.shape[1]

    @pl.when(i == 0)
    def _():
        state_ref[...] = jnp.zeros_like(state_ref)

    tri = tri_ref[...]
    srow = lax.broadcasted_iota(jnp.int32, (GLA_WIDTH, GLA_QK), 0) // GLA_DV
    scol = lax.broadcasted_iota(jnp.int32, (GLA_WIDTH, GLA_QK), 1) // GLA_DK
    same_head = srow == scol
    gn = gn_ref[...]
    for s in range(tm // CHUNK):
        rows = slice(s * CHUNK, (s + 1) * CHUNK)
        row = lax.broadcasted_iota(jnp.int32, (CHUNK, 1), 0) + (i * tm + s * CHUNK)
        real = row >= N_DUMMY
        q = gla_ref[0, rows, 0:GLA_QK]
        k = jnp.where(real, gla_ref[0, rows, GLA_QK:2 * GLA_QK], 0.0)
        v = gla_ref[0, rows, 2 * GLA_QK:2 * GLA_QK + GLA_WIDTH]
        r = gla_ref[0, rows, 2 * GLA_QK + GLA_WIDTH:2 * GLA_QK + 2 * GLA_WIDTH]
        la = jnp.where(real, loga_ref[0, rows, :], 0.0)
        sfx = _tri_sum(tri, la)
        kw = (k * jnp.exp(sfx - la)).astype(BF16)
        decay = jnp.exp(sfx[0:1, :])
        u_t = lax.dot_general(v.astype(BF16), kw, (((0,), (0,)), ((), ())),
                              preferred_element_type=F32)
        state = state_ref[...] * decay + jnp.where(same_head, u_t, 0.0)
        state_ref[...] = state
        o = lax.dot_general(q.astype(BF16), state.astype(BF16), (((1,), (1,)), ((), ())),
                            preferred_element_type=F32)
        outs = []
        for hd in range(GLA_HEADS):
            oh = o[:, hd * GLA_DV:(hd + 1) * GLA_DV]
            ms = jnp.mean(oh * oh, axis=-1, keepdims=True)
            outs.append(oh * lax.rsqrt(ms + EPS))
        y = jnp.concatenate(outs, axis=1) * gn
        gate = r * (1.0 / (1.0 + jnp.exp(-r)))
        o_ref[0, rows, :] = (y * gate).astype(BF16)


def _gla(gla, loga, gn, tri):
    B, LP, _ = gla.shape
    tm = _row_tile(LP, GLA_TILES)
    nt = LP // tm
    return pl.pallas_call(
        _gla_kernel,
        grid=(B, nt),
        in_specs=[pl.BlockSpec((1, tm, _C_FQ), lambda b, i: (b, i, 0)),
                  pl.BlockSpec((1, tm, GLA_QK), lambda b, i: (b, i, 0)),
                  pl.BlockSpec((1, GLA_WIDTH), lambda b, i: (0, 0)),
                  pl.BlockSpec((CHUNK, CHUNK), lambda b, i: (0, 0))],
        out_specs=pl.BlockSpec((1, tm, GLA_WIDTH), lambda b, i: (b, i, 0)),
        out_shape=jax.ShapeDtypeStruct((B, LP, GLA_WIDTH), BF16),
        scratch_shapes=[pltpu.VMEM((GLA_WIDTH, GLA_QK), F32)],
        compiler_params=pltpu.CompilerParams(
            dimension_semantics=("parallel", "arbitrary"), vmem_limit_bytes=VMEM_LIMIT),
        name="gla",
    )(gla, loga, gn, tri)


def _fox_kernel(qa_ref, ka_ref, va_ref, gn_ref, o_ref, sa_ref, sb_ref, m_ref, acc_ref):
    t = pl.program_id(2)
    cols = [slice(hh * LANE, (hh + 1) * LANE) for hh in range(FOX_GROUP)]
    nt = (((1,), (1,)), ((), ()))

    def finish(rows, n):
        lane = lax.broadcasted_iota(jnp.int32, (n, LANE), 1)
        low = lane < FOX_DH
        ys = []
        for hh in range(FOX_GROUP):
            acc = acc_ref[hh, 0:n, :]
            sq = acc * acc
            t = jnp.sum(jnp.where(low, sq * (1.0 / FOX_DH), jnp.where(lane == FOX_DH, sq * EPS, 0.0)),
                        axis=-1, keepdims=True)
            ys.append(acc * lax.rsqrt(t) * gn_ref[0, :, cols[hh]])
        for pr in range(FOX_GROUP // 2):
            o_ref[0, rows, pr * LANE:(pr + 1) * LANE] = jnp.where(
                low, ys[2 * pr], pltpu.roll(ys[2 * pr + 1], FOX_DH, axis=1)).astype(BF16)

    def first_block(q_rows, n, causal):
        for hh in range(FOX_GROUP):
            q = qa_ref[0, q_rows, cols[hh]]
            s = lax.dot_general(q, ka_ref[0, 0:FRONT, cols[hh]], nt, preferred_element_type=F32)
            if causal:
                s = jnp.where(lax.broadcasted_iota(jnp.int32, (n, FRONT), 1)
                              <= lax.broadcasted_iota(jnp.int32, (n, FRONT), 0), s, NEG)
            m = jnp.broadcast_to(jnp.max(s, axis=-1, keepdims=True), (n, LANE))
            p = jnp.exp2(s - m)
            m_ref[hh, 0:n, :] = m
            acc_ref[hh, 0:n, :] = _dot(p.astype(BF16), va_ref[0, 0:FRONT, cols[hh]])

    @pl.when(t == 0)
    def _():
        first_block(slice(0, FRONT), FRONT, causal=True)
        finish(slice(0, FRONT), FRONT)

    @pl.when(t > 0)
    def _():
        r0 = pl.multiple_of(FRONT + (t - 1) * TQ, LANE)

        def q_rows(lo, n):
            return pl.ds(pl.multiple_of(r0 + lo, LANE), n)

        def key_rows(c):
            return pl.ds(pl.multiple_of(FRONT + c * TK, LANE), TK)

        def issue_scores(s_ref, c, lo=0, n=TQ):
            for hh in range(FOX_GROUP):
                s_ref[hh, lo:lo + n, :] = lax.dot_general(
                    qa_ref[0, q_rows(lo, n), cols[hh]], ka_ref[0, key_rows(c), cols[hh]], nt,
                    preferred_element_type=F32)

        def update(s_ref, c, lo=0, n=TQ, causal=False):
            rows = slice(lo, lo + n)
            if causal:
                visible = (lax.broadcasted_iota(jnp.int32, (n, TK), 1)
                           <= lax.broadcasted_iota(jnp.int32, (n, TK), 0))
            for hh in range(FOX_GROUP):
                def scores():
                    s = s_ref[hh, rows, :]
                    return jnp.where(visible, s, NEG) if causal else s

                m = m_ref[hh, rows, :]
                m_new = jnp.maximum(
                    m, jnp.broadcast_to(jnp.max(scores(), axis=-1, keepdims=True), (n, LANE)))
                p = jnp.exp2(scores() - jnp.concatenate([m_new] * (TK // LANE), axis=1))
                acc_ref[hh, rows, :] = (jnp.exp2(m - m_new) * acc_ref[hh, rows, :]
                                        + _dot(p.astype(BF16), va_ref[0, key_rows(c), cols[hh]]))
                m_ref[hh, rows, :] = m_new

        first_block(q_rows(0, TQ), TQ, causal=False)
        issue_scores(sa_ref, 0)

        def open_pair(i, carry):
            issue_scores(sb_ref, 2 * i + 1)
            update(sa_ref, 2 * i)
            issue_scores(sa_ref, 2 * i + 2)
            update(sb_ref, 2 * i + 1)
            return carry

        lax.fori_loop(0, t - 1, open_pair, 0)
        c0 = 2 * (t - 1)
        issue_scores(sb_ref, c0 + 1, lo=TK, n=TQ - TK)
        update(sa_ref, c0, lo=0, n=TK, causal=True)
        update(sa_ref, c0, lo=TK, n=TQ - TK)
        update(sb_ref, c0 + 1, lo=TK, n=TQ - TK, causal=True)
        finish(q_rows(0, TQ), TQ)


def _fox(qa, ka, va, gn):
    B, LP, _ = qa.shape
    gw = FOX_GROUP * LANE
    assert TQ == 2 * TK and (LP - FRONT) % TQ == 0
    seq = lambda n: pl.BlockSpec((1, LP, n), lambda b, h, t: (b, 0, h))
    return pl.pallas_call(
        _fox_kernel,
        grid=(B, FOX_HEADS // FOX_GROUP, 1 + (LP - FRONT) // TQ),
        in_specs=[seq(gw), seq(gw), seq(gw), pl.BlockSpec((1, 1, gw), lambda b, h, t: (h, 0, 0))],
        out_specs=seq(gw // 2),
        out_shape=jax.ShapeDtypeStruct((B, LP, FOX_WIDTH), BF16),
        scratch_shapes=[pltpu.VMEM((FOX_GROUP, TQ, TK), F32), pltpu.VMEM((FOX_GROUP, TQ, TK), F32),
                        pltpu.VMEM((FOX_GROUP, TQ, LANE), F32), pltpu.VMEM((FOX_GROUP, TQ, LANE), F32)],
        compiler_params=pltpu.CompilerParams(
            dimension_semantics=("parallel", "parallel", "arbitrary"), vmem_limit_bytes=VMEM_LIMIT),
        name="fox",
    )(qa, ka, va, gn)


def _ffn_kernel(h_ref, og_ref, of_ref, wog_ref, wof_ref, g_ref, wu_ref, wg_ref, cw_ref, cb_ref,
                wd_ref, fin_ref, out_ref, xn_ref, acc_ref, ha_ref, hb_ref, halo_ref, *, final):
    i = pl.program_id(1)
    tm = h_ref.shape[1]

    @pl.when(i == 0)
    def _():
        halo_ref[...] = jnp.zeros_like(halo_ref)

    h1 = h_ref[0] + (_dot(og_ref[0], wog_ref[...]) + _dot(of_ref[0], wof_ref[...]))
    acc_ref[...] = h1
    ms = jnp.mean(h1 * h1, axis=-1, keepdims=True)
    row = lax.broadcasted_iota(jnp.int32, (tm, 1), 0) + i * tm
    xn = jnp.where(row >= N_DUMMY, h1 * lax.rsqrt(ms + EPS) * g_ref[...], 0.0)
    xn_ref[...] = xn.astype(BF16)

    def up(buf_ref, j):
        xnb = xn_ref[...]
        buf_ref[0, SUBLANE:SUBLANE + tm, :] = _dot(xnb, wu_ref[j])
        buf_ref[1, SUBLANE:SUBLANE + tm, :] = _dot(xnb, wg_ref[j])

    def conv(buf_ref, half, j):
        buf_ref[half, 0:SUBLANE, :] = halo_ref[half, j]
        halo_ref[half, j] = buf_ref[half, tm:tm + SUBLANE, :]
        w = cw_ref[half, j]
        out = cb_ref[half, j][0:1, :] + buf_ref[half, SUBLANE:SUBLANE + tm, :] * w[2:3, :]
        out = out + buf_ref[half, SUBLANE - 1:SUBLANE - 1 + tm, :] * w[1:2, :]
        return out + buf_ref[half, SUBLANE - 2:SUBLANE - 2 + tm, :] * w[0:1, :]

    def down(buf_ref, j):
        cu = conv(buf_ref, 0, j)
        cg = conv(buf_ref, 1, j)
        a = (cg * (1.0 / (1.0 + jnp.exp(-cg))) * cu).astype(BF16)
        acc_ref[...] += _dot(a, wd_ref[j])

    def pair(p, carry):
        up(hb_ref, 2 * p + 1)
        down(ha_ref, 2 * p)
        up(ha_ref, 2 * p + 2)
        down(hb_ref, 2 * p + 1)
        return carry

    assert N_FF_CHUNKS % 2 == 1
    up(ha_ref, 0)
    lax.fori_loop(0, N_FF_CHUNKS // 2, pair, 0)
    down(ha_ref, N_FF_CHUNKS - 1)
    y = acc_ref[...]
    if final:
        ms = jnp.mean(y * y, axis=-1, keepdims=True)
        y = y * lax.rsqrt(ms + EPS) * fin_ref[...]
    out_ref[0] = y


def _ffn(h, og, of, wog, wof, g, wu, wg, cw, cb, wd, fin, final):
    B, LP, D = h.shape
    tm = _row_tile(LP)
    nt = LP // tm
    resident = lambda shape: pl.BlockSpec(shape, lambda b, i: (0,) * len(shape),
                                          pipeline_mode=pl.Buffered(1))
    row_blk = lambda n: pl.BlockSpec((1, tm, n), lambda b, i: (b, i, 0))
    return pl.pallas_call(
        functools.partial(_ffn_kernel, final=final),
        grid=(B, nt),
        in_specs=[row_blk(D), row_blk(GLA_WIDTH), row_blk(FOX_WIDTH),
                  resident((GLA_WIDTH, D)), resident((FOX_WIDTH, D)), resident((1, D)),
                  resident((N_FF_CHUNKS, D, FF_CHUNK)), resident((N_FF_CHUNKS, D, FF_CHUNK)),
                  resident((2, N_FF_CHUNKS, SUBLANE, FF_CHUNK)),
                  resident((2, N_FF_CHUNKS, SUBLANE, FF_CHUNK)),
                  resident((N_FF_CHUNKS, FF_CHUNK, D)), resident((1, D))],
        out_specs=row_blk(D),
        out_shape=jax.ShapeDtypeStruct((B, LP, D), F32),
        scratch_shapes=[pltpu.VMEM((tm, D), BF16), pltpu.VMEM((tm, D), F32),
                        pltpu.VMEM((2, tm + SUBLANE, FF_CHUNK), F32),
                        pltpu.VMEM((2, tm + SUBLANE, FF_CHUNK), F32),
                        pltpu.VMEM((2, N_FF_CHUNKS, SUBLANE, FF_CHUNK), F32)],
        compiler_params=pltpu.CompilerParams(
            dimension_semantics=("parallel", "arbitrary"), vmem_limit_bytes=VMEM_LIMIT),
        name="ffn",
    )(h, og, of, wog, wof, g, wu, wg, cw, cb, wd, fin)


def _selection_matrices():
    sq = np.zeros((LANE, FOX_HEADS * LANE), np.float32)
    sk = np.zeros((LANE, FOX_HEADS * LANE), np.float32)
    one = 3 * FOX_HEADS
    for h in range(FOX_HEADS):
        base = h * LANE + FOX_DH
        for p in range(3):
            sq[p * FOX_HEADS + h, base + p] = 1.0
            sk[one, base + p] = 1.0
            sq[one, base + 3 + p] = 1.0
            sk[p * FOX_HEADS + h, base + 3 + p] = -1.0
    return jnp.asarray(sq, BF16), jnp.asarray(sk, BF16)


def _chunk_cols(w):
    return jnp.transpose(w.reshape(w.shape[0], N_FF_CHUNKS, FF_CHUNK), (1, 0, 2))


def _pad_rows(w):
    w = jnp.pad(w, ((0, SUBLANE - w.shape[0]), (0, 0)))
    return _chunk_cols(w)


def kernel(x, meta_tokens, attn_norm, w_in, w_alpha_up, b_alpha, b_forget, gla_norm, fox_norm,
           w_out, ffn_norm, w_up, conv_w, conv_b, w_down, final_norm):
    B, S, D = x.shape
    depth = w_in.shape[0]
    assert D == D_MODEL and S % TQ == 0
    meta = jnp.broadcast_to(meta_tokens[None].astype(x.dtype), (B, N_META, D))
    h = jnp.concatenate([jnp.zeros((B, N_DUMMY, D), x.dtype), meta, x], axis=1)

    sq, sk = _selection_matrices()
    tri_lo = jnp.asarray(np.tril(np.ones((CHUNK, CHUNK), np.float32)), BF16)
    tri_up = jnp.asarray(np.triu(np.ones((CHUNK, CHUNK), np.float32)), BF16)
    c = np.cumsum((GLA_QK, GLA_QK, GLA_WIDTH, GLA_WIDTH, GLA_LOWRANK, FOX_WIDTH, FOX_WIDTH,
                   FOX_WIDTH, FOX_HEADS))

    for l in range(depth):
        w = w_in[l]
        w_main = jnp.concatenate(
            [w[:, :c[0]] * GLA_DK ** -0.5, w[:, c[0]:c[3]], w[:, c[4]:c[7]], w[:, c[7]:c[8]],
             w[:, c[3]:c[4]], jnp.zeros((D, LANE - FOX_HEADS - GLA_LOWRANK), w.dtype)],
            axis=1).astype(BF16)
        wa = jnp.zeros((LANE, GLA_QK), F32).at[FOX_HEADS:FOX_HEADS + GLA_LOWRANK].set(
            w_alpha_up[l]).astype(BF16)
        bf = jnp.zeros((1, LANE), F32).at[0, :FOX_HEADS].set(b_forget[l])
        gla, loga, qa, ka, va = _inproj(h, attn_norm[l][None], w_main, wa, b_alpha[l][None], bf,
                                        sq, sk, tri_lo)
        o_gla = _gla(gla, loga, gla_norm[l][None], tri_up)
        fgn = jnp.pad(fox_norm[l].reshape(FOX_HEADS, FOX_DH), ((0, 0), (0, LANE - FOX_DH)))
        o_fox = _fox(qa, ka, va, fgn.reshape(FOX_HEADS // FOX_GROUP, 1, FOX_GROUP * LANE))
        wo = w_out[l].astype(BF16)
        cw = jnp.stack([_pad_rows(conv_w[l][:, :D_FF]), _pad_rows(conv_w[l][:, D_FF:])])
        cb = jnp.stack([_pad_rows(conv_b[l][None, :D_FF]), _pad_rows(conv_b[l][None, D_FF:])])
        h = _ffn(h, o_gla, o_fox, wo[:GLA_WIDTH], wo[GLA_WIDTH:], ffn_norm[l][None],
                 _chunk_cols(w_up[l][:, :D_FF]).astype(BF16),
                 _chunk_cols(w_up[l][:, D_FF:]).astype(BF16), cw, cb,
                 w_down[l].reshape(N_FF_CHUNKS, FF_CHUNK, D).astype(BF16), final_norm[None],
                 final=(l == depth - 1))
    return h[:, FRONT:]
```

```python
import functools

import jax
import jax.numpy as jnp
import numpy as np
from jax import lax
from jax.experimental import pallas as pl
from jax.experimental.pallas import tpu as pltpu

D_MODEL = 1024
N_META = 16
CHUNK = 64
EPS = 1e-6
GLA_HEADS = 4
GLA_DK = 64
GLA_DV = 128
GLA_LOWRANK = 16
GLA_TAU = 16.0
GLA_QK = GLA_HEADS * GLA_DK
GLA_WIDTH = GLA_HEADS * GLA_DV
FOX_HEADS = 8
FOX_DH = 64
FOX_WIDTH = FOX_HEADS * FOX_DH
D_FF = 2816
CONV_W = 3

LANE = 128
SUBLANE = 8
FRONT = 128
N_DUMMY = FRONT - N_META
ROW_TILES = (704, 384, 128)
TQ = 1024
TK = 512
FOX_GROUP = 2
FF_CHUNK = 256
N_FF_CHUNKS = D_FF // FF_CHUNK
LOG2E = 1.4426950408889634
KEY_MASK = 16384.0
NEG = -1e30
VMEM_LIMIT = 56 * 1024 * 1024

_C_GLA = 0
_C_FQ = 1536
_C_FK = 2048
_C_FV = 2560
_C_SM = 3072
_N_MAIN = 3200

F32 = jnp.float32
BF16 = jnp.bfloat16


def _dot(a, b):
    return jnp.dot(a, b, preferred_element_type=F32)


def _log_sigmoid(x):
    return jnp.minimum(x, 0.0) - jnp.log1p(jnp.exp(-jnp.abs(x)))


def _split3(x):
    p1 = x.astype(BF16).astype(F32)
    r1 = x - p1
    p2 = r1.astype(BF16).astype(F32)
    p3 = (r1 - p2).astype(BF16).astype(F32)
    return p1, p2, p3


def _tri_sum(tri, x):
    p1, p2, p3 = _split3(x)
    return (_dot(tri, p1.astype(BF16)) + _dot(tri, p2.astype(BF16))) + _dot(tri, p3.astype(BF16))


def _gla_chunks(i, tm, gla_ref, loga_ref, gn_ref, tri_ref, o_ref, state_ref, between_chunks):
    tri = tri_ref[...]
    srow = lax.broadcasted_iota(jnp.int32, (GLA_WIDTH, GLA_QK), 0) // GLA_DV
    scol = lax.broadcasted_iota(jnp.int32, (GLA_WIDTH, GLA_QK), 1) // GLA_DK
    same_head = srow == scol
    gn = gn_ref[...]
    n = tm // CHUNK

    def chunk_rows(s):
        rows = slice(s * CHUNK, (s + 1) * CHUNK)
        row = lax.broadcasted_iota(jnp.int32, (CHUNK, 1), 0) + (i * tm + s * CHUNK)
        return rows, row >= N_DUMMY

    def decays(s):
        rows, real = chunk_rows(s)
        la = jnp.where(real, loga_ref[rows, :], 0.0)
        return la, _tri_sum(tri, la)

    def summary(s, la, sfx):
        rows, real = chunk_rows(s)
        k = jnp.where(real, gla_ref[rows, GLA_QK:2 * GLA_QK], 0.0)
        v = gla_ref[rows, 2 * GLA_QK:2 * GLA_QK + GLA_WIDTH]
        kw = (k * jnp.exp(sfx - la)).astype(BF16)
        u_t = lax.dot_general(v.astype(BF16), kw, (((0,), (0,)), ((), ())),
                              preferred_element_type=F32)
        return jnp.exp(sfx[0:1, :]), u_t

    def readout(s, state, decay, u_t):
        rows, _ = chunk_rows(s)
        q = gla_ref[rows, 0:GLA_QK]
        r = gla_ref[rows, 2 * GLA_QK + GLA_WIDTH:2 * GLA_QK + 2 * GLA_WIDTH]
        state = state * decay + jnp.where(same_head, u_t, 0.0)
        o = lax.dot_general(q.astype(BF16), state.astype(BF16), (((1,), (1,)), ((), ())),
                            preferred_element_type=F32)
        outs = []
        for hd in range(GLA_HEADS):
            oh = o[:, hd * GLA_DV:(hd + 1) * GLA_DV]
            ms = jnp.mean(oh * oh, axis=-1, keepdims=True)
            outs.append(oh * lax.rsqrt(ms + EPS))
        y = jnp.concatenate(outs, axis=1) * gn
        gate = r * (1.0 / (1.0 + jnp.exp(-r)))
        o_ref[0, rows, :] = (y * gate).astype(BF16)
        return state

    state = state_ref[...]
    dec = {s: decays(s) for s in range(min(2, n))}
    summ = {0: summary(0, *dec[0])}
    for s in range(n):
        if s + 2 < n:
            dec[s + 2] = decays(s + 2)
        if s + 1 < n:
            summ[s + 1] = summary(s + 1, *dec.pop(s + 1))
        state = readout(s, state, *summ.pop(s))
        between_chunks(s)
    state_ref[...] = state


def _inproj_kernel(h_ref, g_ref, w_ref, wa_ref, ba_ref, bf_ref, sq_ref, sk_ref, tri_ref, gn_ref,
                   triu_ref, og_ref, qa_ref, ka_ref, va_ref, carry_ref, gla_ref, loga_ref,
                   state_ref):
    i = pl.program_id(1)
    tm = h_ref.shape[1]
    x = h_ref[0]
    ms = jnp.mean(x * x, axis=-1, keepdims=True)
    xn = (x * lax.rsqrt(ms + EPS) * g_ref[...]).astype(BF16)

    @pl.when(i == 0)
    def _():
        carry_ref[...] = jnp.zeros_like(carry_ref)
        state_ref[...] = jnp.zeros_like(state_ref)

    gla_ref[...] = _dot(xn, w_ref[:, _C_GLA:_C_FQ])
    sm = _dot(xn, w_ref[:, _C_SM:_N_MAIN])
    z = _dot(sm.astype(BF16), wa_ref[...]) + ba_ref[...]
    loga_ref[...] = _log_sigmoid(z) * (1.0 / GLA_TAU)

    fox_cols = list(range(_C_FQ, _C_SM, 2 * LANE))
    fox_blocks = []

    def project_fox_block(s):
        if s % 2 == 1 and len(fox_blocks) < len(fox_cols):
            c0 = fox_cols[len(fox_blocks)]
            fox_blocks.append(_dot(xn, w_ref[:, c0:c0 + 2 * LANE]))

    _gla_chunks(i, tm, gla_ref, loga_ref, gn_ref, triu_ref, og_ref, state_ref, project_fox_block)
    while len(fox_blocks) < len(fox_cols):
        project_fox_block(1)
    fq, fk, fv = (jnp.concatenate(fox_blocks[j:j + 2], axis=1) for j in (0, 2, 4))

    lane = lax.broadcasted_iota(jnp.int32, (tm, LANE), 1)
    row = lax.broadcasted_iota(jnp.int32, (tm, LANE), 0) + i * tm
    real = row >= N_DUMMY
    head_lane = lane < FOX_HEADS
    lf = jnp.where(head_lane & real, _log_sigmoid(sm + bf_ref[...]), 0.0)

    carry = carry_ref[0:1, :]
    tri = tri_ref[...]
    blocks = []
    for j in range(tm // CHUNK):
        cs = _tri_sum(tri, lf[j * CHUNK:(j + 1) * CHUNK]) + carry
        carry = cs[CHUNK - 1:CHUNK, :]
        blocks.append(cs)
    carry_ref[...] = jnp.broadcast_to(carry, carry_ref.shape)
    c = jnp.concatenate(blocks, axis=0) * LOG2E
    c = jnp.where(head_lane, jnp.where(real, c, KEY_MASK), 0.0)
    c1, c2, c3 = _split3(c)
    cc = (c1 + pltpu.roll(c2, FOX_HEADS, axis=1)) + (pltpu.roll(c3, 2 * FOX_HEADS, axis=1)
                                                      + jnp.where(lane == 3 * FOX_HEADS, 1.0, 0.0))
    cc = cc.astype(BF16)
    augq = _dot(cc, sq_ref[...])
    augk = _dot(cc, sk_ref[...])

    low = lane < FOX_DH
    ones_col = jnp.where(lane == FOX_DH, 1.0, 0.0)
    for h in range(FOX_HEADS):
        pair = slice((h // 2) * LANE, (h // 2 + 1) * LANE)
        blk = slice(h * LANE, (h + 1) * LANE)
        xq, xk, xv = fq[:, pair], fk[:, pair], fv[:, pair]
        if h % 2:
            xq = pltpu.roll(xq, FOX_DH, axis=1)
            xk = pltpu.roll(xk, FOX_DH, axis=1)
            xv = pltpu.roll(xv, FOX_DH, axis=1)
        qa_ref[0, :, blk] = jnp.where(low, xq * (LOG2E * FOX_DH ** -0.5), augq[:, blk]).astype(BF16)
        ka_ref[0, :, blk] = jnp.where(low, xk, augk[:, blk]).astype(BF16)
        va_ref[0, :, blk] = jnp.where(low, xv, ones_col).astype(BF16)


def _row_tile(lp, candidates=ROW_TILES):
    return next(tm for tm in candidates if lp % tm == 0)


def _inproj(h, g, w_main, wa, ba, bf, sq, sk, tri_lo, gn, tri_up):
    B, LP, D = h.shape
    tm = _row_tile(LP)
    nt = LP // tm
    const = lambda shape: pl.BlockSpec(shape, lambda b, i: (0,) * len(shape),
                                       pipeline_mode=pl.Buffered(1))
    row_blk = lambda n: pl.BlockSpec((1, tm, n), lambda b, i: (b, i, 0))
    return pl.pallas_call(
        _inproj_kernel,
        grid=(B, nt),
        in_specs=[row_blk(D), const((1, D)), const((D, _N_MAIN)), const((LANE, GLA_QK)),
                  const((1, GLA_QK)), const((1, LANE)), const((LANE, FOX_HEADS * LANE)),
                  const((LANE, FOX_HEADS * LANE)), const((CHUNK, CHUNK)), const((1, GLA_WIDTH)),
                  const((CHUNK, CHUNK))],
        out_specs=[row_blk(GLA_WIDTH), row_blk(FOX_HEADS * LANE),
                   row_blk(FOX_HEADS * LANE), row_blk(FOX_HEADS * LANE)],
        out_shape=[jax.ShapeDtypeStruct((B, LP, GLA_WIDTH), BF16),
                   jax.ShapeDtypeStruct((B, LP, FOX_HEADS * LANE), BF16),
                   jax.ShapeDtypeStruct((B, LP, FOX_HEADS * LANE), BF16),
                   jax.ShapeDtypeStruct((B, LP, FOX_HEADS * LANE), BF16)],
        scratch_shapes=[pltpu.VMEM((SUBLANE, LANE), F32), pltpu.VMEM((tm, _C_FQ), F32),
                        pltpu.VMEM((tm, GLA_QK), F32), pltpu.VMEM((GLA_WIDTH, GLA_QK), F32)],
        compiler_params=pltpu.CompilerParams(
            dimension_semantics=("parallel", "arbitrary"), vmem_limit_bytes=VMEM_LIMIT),
        name="inproj",
    )(h, g, w_main, wa, ba, bf, sq, sk, tri_lo, gn, tri_up)


def _fox_kernel(qa_ref, ka_ref, va_ref, gn_ref, o_ref, sa_ref, sb_ref, m_ref, acc_ref):
    t = pl.program_id(2)
    cols = [slice(hh * LANE, (hh + 1) * LANE) for hh in range(FOX_GROUP)]
    nt = (((1,), (1,)), ((), ()))

    def finish(rows, n):
        lane = lax.broadcasted_iota(jnp.int32, (n, LANE), 1)
        low = lane < FOX_DH
        ys = []
        for hh in range(FOX_GROUP):
            acc = acc_ref[hh, 0:n, :]
            sq = acc * acc
            t = jnp.sum(jnp.where(low, sq * (1.0 / FOX_DH), jnp.where(lane == FOX_DH, sq * EPS, 0.0)),
                        axis=-1, keepdims=True)
            ys.append(acc * lax.rsqrt(t) * gn_ref[0, :, cols[hh]])
        for pr in range(FOX_GROUP // 2):
            o_ref[0, rows, pr * LANE:(pr + 1) * LANE] = jnp.where(
                low, ys[2 * pr], pltpu.roll(ys[2 * pr + 1], FOX_DH, axis=1)).astype(BF16)

    def first_block(q_rows, n, causal):
        for hh in range(FOX_GROUP):
            q = qa_ref[0, q_rows, cols[hh]]
            s = lax.dot_general(q, ka_ref[0, 0:FRONT, cols[hh]], nt, preferred_element_type=F32)
            if causal:
                s = jnp.where(lax.broadcasted_iota(jnp.int32, (n, FRONT), 1)
                              <= lax.broadcasted_iota(jnp.int32, (n, FRONT), 0), s, NEG)
            m = jnp.broadcast_to(jnp.max(s, axis=-1, keepdims=True), (n, LANE))
            p = jnp.exp2(s - m)
            m_ref[hh, 0:n, :] = m
            acc_ref[hh, 0:n, :] = _dot(p.astype(BF16), va_ref[0, 0:FRONT, cols[hh]])

    @pl.when(t == 0)
    def _():
        first_block(slice(0, FRONT), FRONT, causal=True)
        finish(slice(0, FRONT), FRONT)

    @pl.when(t > 0)
    def _():
        r0 = pl.multiple_of(FRONT + (t - 1) * TQ, LANE)

        def q_rows(lo, n):
            return pl.ds(pl.multiple_of(r0 + lo, LANE), n)

        def key_rows(c):
            return pl.ds(pl.multiple_of(FRONT + c * TK, LANE), TK)

        def issue_scores(s_ref, c, lo=0, n=TQ):
            for hh in range(FOX_GROUP):
                s_ref[hh, lo:lo + n, :] = lax.dot_general(
                    qa_ref[0, q_rows(lo, n), cols[hh]], ka_ref[0, key_rows(c), cols[hh]], nt,
                    preferred_element_type=F32)

        def update(s_ref, c, lo=0, n=TQ, causal=False):
            rows = slice(lo, lo + n)
            if causal:
                visible = (lax.broadcasted_iota(jnp.int32, (n, TK), 1)
                           <= lax.broadcasted_iota(jnp.int32, (n, TK), 0))
            for hh in range(FOX_GROUP):
                def scores():
                    s = s_ref[hh, rows, :]
                    return jnp.where(visible, s, NEG) if causal else s

                m = m_ref[hh, rows, :]
                m_new = jnp.maximum(
                    m, jnp.broadcast_to(jnp.max(scores(), axis=-1, keepdims=True), (n, LANE)))
                p = jnp.exp2(scores() - jnp.concatenate([m_new] * (TK // LANE), axis=1))
                acc_ref[hh, rows, :] = (jnp.exp2(m - m_new) * acc_ref[hh, rows, :]
                                        + _dot(p.astype(BF16), va_ref[0, key_rows(c), cols[hh]]))
                m_ref[hh, rows, :] = m_new

        first_block(q_rows(0, TQ), TQ, causal=False)
        issue_scores(sa_ref, 0)

        def open_pair(i, carry):
            issue_scores(sb_ref, 2 * i + 1)
            update(sa_ref, 2 * i)
            issue_scores(sa_ref, 2 * i + 2)
            update(sb_ref, 2 * i + 1)
            return carry

        lax.fori_loop(0, t - 1, open_pair, 0)
        c0 = 2 * (t - 1)
        issue_scores(sb_ref, c0 + 1, lo=TK, n=TQ - TK)
        update(sa_ref, c0, lo=0, n=TK, causal=True)
        update(sa_ref, c0, lo=TK, n=TQ - TK)
        update(sb_ref, c0 + 1, lo=TK, n=TQ - TK, causal=True)
        finish(q_rows(0, TQ), TQ)


def _fox(qa, ka, va, gn):
    B, LP, _ = qa.shape
    gw = FOX_GROUP * LANE
    assert TQ == 2 * TK and (LP - FRONT) % TQ == 0
    seq = lambda n: pl.BlockSpec((1, LP, n), lambda b, h, t: (b, 0, h))
    return pl.pallas_call(
        _fox_kernel,
        grid=(B, FOX_HEADS // FOX_GROUP, 1 + (LP - FRONT) // TQ),
        in_specs=[seq(gw), seq(gw), seq(gw), pl.BlockSpec((1, 1, gw), lambda b, h, t: (h, 0, 0))],
        out_specs=seq(gw // 2),
        out_shape=jax.ShapeDtypeStruct((B, LP, FOX_WIDTH), BF16),
        scratch_shapes=[pltpu.VMEM((FOX_GROUP, TQ, TK), F32), pltpu.VMEM((FOX_GROUP, TQ, TK), F32),
                        pltpu.VMEM((FOX_GROUP, TQ, LANE), F32), pltpu.VMEM((FOX_GROUP, TQ, LANE), F32)],
        compiler_params=pltpu.CompilerParams(
            dimension_semantics=("parallel", "parallel", "arbitrary"), vmem_limit_bytes=VMEM_LIMIT),
        name="fox",
    )(qa, ka, va, gn)


def _ffn_kernel(h_ref, og_ref, of_ref, wog_ref, wof_ref, g_ref, wu_ref, wg_ref, cw_ref, cb_ref,
                wd_ref, fin_ref, out_ref, xn_ref, acc_ref, ha_ref, hb_ref, halo_ref, *, final):
    i = pl.program_id(1)
    tm = h_ref.shape[1]

    @pl.when(i == 0)
    def _():
        halo_ref[...] = jnp.zeros_like(halo_ref)

    h1 = h_ref[0] + (_dot(og_ref[0], wog_ref[...]) + _dot(of_ref[0], wof_ref[...]))
    acc_ref[...] = h1
    ms = jnp.mean(h1 * h1, axis=-1, keepdims=True)
    row = lax.broadcasted_iota(jnp.int32, (tm, 1), 0) + i * tm
    xn = jnp.where(row >= N_DUMMY, h1 * lax.rsqrt(ms + EPS) * g_ref[...], 0.0)
    xn_ref[...] = xn.astype(BF16)

    def up(buf_ref, j):
        xnb = xn_ref[...]
        buf_ref[0, SUBLANE:SUBLANE + tm, :] = _dot(xnb, wu_ref[j])
        buf_ref[1, SUBLANE:SUBLANE + tm, :] = _dot(xnb, wg_ref[j])

    def conv(buf_ref, half, j):
        buf_ref[half, 0:SUBLANE, :] = halo_ref[half, j]
        halo_ref[half, j] = buf_ref[half, tm:tm + SUBLANE, :]
        w = cw_ref[half, j]
        out = cb_ref[half, j][0:1, :] + buf_ref[half, SUBLANE:SUBLANE + tm, :] * w[2:3, :]
        out = out + buf_ref[half, SUBLANE - 1:SUBLANE - 1 + tm, :] * w[1:2, :]
        return out + buf_ref[half, SUBLANE - 2:SUBLANE - 2 + tm, :] * w[0:1, :]

    def down(buf_ref, j):
        cu = conv(buf_ref, 0, j)
        cg = conv(buf_ref, 1, j)
        a = (cg * (1.0 / (1.0 + jnp.exp(-cg))) * cu).astype(BF16)
        acc_ref[...] += _dot(a, wd_ref[j])

    def pair(p, carry):
        up(hb_ref, 2 * p + 1)
        down(ha_ref, 2 * p)
        up(ha_ref, 2 * p + 2)
        down(hb_ref, 2 * p + 1)
        return carry

    assert N_FF_CHUNKS % 2 == 1
    up(ha_ref, 0)
    lax.fori_loop(0, N_FF_CHUNKS // 2, pair, 0)
    down(ha_ref, N_FF_CHUNKS - 1)
    y = acc_ref[...]
    if final:
        ms = jnp.mean(y * y, axis=-1, keepdims=True)
        y = y * lax.rsqrt(ms + EPS) * fin_ref[...]
    out_ref[0] = y


def _ffn(h, og, of, wog, wof, g, wu, wg, cw, cb, wd, fin, final):
    B, LP, D = h.shape
    tm = _row_tile(LP)
    nt = LP // tm
    resident = lambda shape: pl.BlockSpec(shape, lambda b, i: (0,) * len(shape),
                                          pipeline_mode=pl.Buffered(1))
    row_blk = lambda n: pl.BlockSpec((1, tm, n), lambda b, i: (b, i, 0))
    return pl.pallas_call(
        functools.partial(_ffn_kernel, final=final),
        grid=(B, nt),
        in_specs=[row_blk(D), row_blk(GLA_WIDTH), row_blk(FOX_WIDTH),
                  resident((GLA_WIDTH, D)), resident((FOX_WIDTH, D)), resident((1, D)),
                  resident((N_FF_CHUNKS, D, FF_CHUNK)), resident((N_FF_CHUNKS, D, FF_CHUNK)),
                  resident((2, N_FF_CHUNKS, SUBLANE, FF_CHUNK)),
                  resident((2, N_FF_CHUNKS, SUBLANE, FF_CHUNK)),
                  resident((N_FF_CHUNKS, FF_CHUNK, D)), resident((1, D))],
        out_specs=row_blk(D),
        out_shape=jax.ShapeDtypeStruct((B, LP, D), F32),
        scratch_shapes=[pltpu.VMEM((tm, D), BF16), pltpu.VMEM((tm, D), F32),
                        pltpu.VMEM((2, tm + SUBLANE, FF_CHUNK), F32),
                        pltpu.VMEM((2, tm + SUBLANE, FF_CHUNK), F32),
                        pltpu.VMEM((2, N_FF_CHUNKS, SUBLANE, FF_CHUNK), F32)],
        compiler_params=pltpu.CompilerParams(
            dimension_semantics=("parallel", "arbitrary"), vmem_limit_bytes=VMEM_LIMIT),
        name="ffn",
    )(h, og, of, wog, wof, g, wu, wg, cw, cb, wd, fin)


def _selection_matrices():
    sq = np.zeros((LANE, FOX_HEADS * LANE), np.float32)
    sk = np.zeros((LANE, FOX_HEADS * LANE), np.float32)
    one = 3 * FOX_HEADS
    for h in range(FOX_HEADS):
        base = h * LANE + FOX_DH
        for p in range(3):
            sq[p * FOX_HEADS + h, base + p] = 1.0
            sk[one, base + p] = 1.0
            sq[one, base + 3 + p] = 1.0
            sk[p * FOX_HEADS + h, base + 3 + p] = -1.0
    return jnp.asarray(sq, BF16), jnp.asarray(sk, BF16)


def _chunk_cols(w):
    return jnp.transpose(w.reshape(w.shape[0], N_FF_CHUNKS, FF_CHUNK), (1, 0, 2))


def _pad_rows(w):
    w = jnp.pad(w, ((0, SUBLANE - w.shape[0]), (0, 0)))
    return _chunk_cols(w)


def kernel(x, meta_tokens, attn_norm, w_in, w_alpha_up, b_alpha, b_forget, gla_norm, fox_norm,
           w_out, ffn_norm, w_up, conv_w, conv_b, w_down, final_norm):
    B, S, D = x.shape
    depth = w_in.shape[0]
    assert D == D_MODEL and S % TQ == 0
    meta = jnp.broadcast_to(meta_tokens[None].astype(x.dtype), (B, N_META, D))
    h = jnp.concatenate([jnp.zeros((B, N_DUMMY, D), x.dtype), meta, x], axis=1)

    sq, sk = _selection_matrices()
    tri_lo = jnp.asarray(np.tril(np.ones((CHUNK, CHUNK), np.float32)), BF16)
    tri_up = jnp.asarray(np.triu(np.ones((CHUNK, CHUNK), np.float32)), BF16)
    c = np.cumsum((GLA_QK, GLA_QK, GLA_WIDTH, GLA_WIDTH, GLA_LOWRANK, FOX_WIDTH, FOX_WIDTH,
                   FOX_WIDTH, FOX_HEADS))

    for l in range(depth):
        w = w_in[l]
        w_main = jnp.concatenate(
            [w[:, :c[0]] * GLA_DK ** -0.5, w[:, c[0]:c[3]], w[:, c[4]:c[7]], w[:, c[7]:c[8]],
             w[:, c[3]:c[4]], jnp.zeros((D, LANE - FOX_HEADS - GLA_LOWRANK), w.dtype)],
            axis=1).astype(BF16)
        wa = jnp.zeros((LANE, GLA_QK), F32).at[FOX_HEADS:FOX_HEADS + GLA_LOWRANK].set(
            w_alpha_up[l]).astype(BF16)
        bf = jnp.zeros((1, LANE), F32).at[0, :FOX_HEADS].set(b_forget[l])
        o_gla, qa, ka, va = _inproj(h, attn_norm[l][None], w_main, wa, b_alpha[l][None], bf,
                                    sq, sk, tri_lo, gla_norm[l][None], tri_up)
        fgn = jnp.pad(fox_norm[l].reshape(FOX_HEADS, FOX_DH), ((0, 0), (0, LANE - FOX_DH)))
        o_fox = _fox(qa, ka, va, fgn.reshape(FOX_HEADS // FOX_GROUP, 1, FOX_GROUP * LANE))
        wo = w_out[l].astype(BF16)
        cw = jnp.stack([_pad_rows(conv_w[l][:, :D_FF]), _pad_rows(conv_w[l][:, D_FF:])])
        cb = jnp.stack([_pad_rows(conv_b[l][None, :D_FF]), _pad_rows(conv_b[l][None, D_FF:])])
        h = _ffn(h, o_gla, o_fox, wo[:GLA_WIDTH], wo[GLA_WIDTH:], ffn_norm[l][None],
                 _chunk_cols(w_up[l][:, :D_FF]).astype(BF16),
                 _chunk_cols(w_up[l][:, D_FF:]).astype(BF16), cw, cb,
                 w_down[l].reshape(N_FF_CHUNKS, FF_CHUNK, D).astype(BF16), final_norm[None],
                 final=(l == depth - 1))
    return h[:, FRONT:]
```

```python
import functools

import jax
import jax.numpy as jnp
import numpy as np
from jax import lax
from jax.experimental import pallas as pl
from jax.experimental.pallas import tpu as pltpu

D_MODEL = 1024
N_META = 16
CHUNK = 64
EPS = 1e-6
GLA_HEADS = 4
GLA_DK = 64
GLA_DV = 128
GLA_LOWRANK = 16
GLA_TAU = 16.0
GLA_QK = GLA_HEADS * GLA_DK
GLA_WIDTH = GLA_HEADS * GLA_DV
FOX_HEADS = 8
FOX_DH = 64
FOX_WIDTH = FOX_HEADS * FOX_DH
D_FF = 2816
CONV_W = 3

LANE = 128
SUBLANE = 8
FRONT = 128
N_DUMMY = FRONT - N_META
ROW_TILES = (704, 384, 128)
TQ = 1024
TK = 512
FOX_GROUP = 2
FF_CHUNK = 256
N_FF_CHUNKS = D_FF // FF_CHUNK
LOG2E = 1.4426950408889634
KEY_MASK = 16384.0
NEG = -1e30
VMEM_LIMIT = 56 * 1024 * 1024

_C_GLA = 0
_C_FQ = 1536
_C_FK = 2048
_C_FV = 2560
_C_SM = 3072
_N_MAIN = 3200

F32 = jnp.float32
BF16 = jnp.bfloat16


def _dot(a, b):
    return jnp.dot(a, b, preferred_element_type=F32)


def _log_sigmoid(x):
    return jnp.minimum(x, 0.0) - jnp.log1p(jnp.exp(-jnp.abs(x)))


def _split3(x):
    p1 = x.astype(BF16).astype(F32)
    r1 = x - p1
    p2 = r1.astype(BF16).astype(F32)
    p3 = (r1 - p2).astype(BF16).astype(F32)
    return p1, p2, p3


def _tri_sum(tri, x):
    p1, p2, p3 = _split3(x)
    return (_dot(tri, p1.astype(BF16)) + _dot(tri, p2.astype(BF16))) + _dot(tri, p3.astype(BF16))


def _gla_chunks(i, tm, gla_ref, loga_ref, gn_ref, tri_ref, o_ref, state_ref, between_chunks):
    tri = tri_ref[...]
    srow = lax.broadcasted_iota(jnp.int32, (GLA_WIDTH, GLA_QK), 0) // GLA_DV
    scol = lax.broadcasted_iota(jnp.int32, (GLA_WIDTH, GLA_QK), 1) // GLA_DK
    same_head = srow == scol
    gn = gn_ref[...]
    n = tm // CHUNK

    def chunk_rows(s):
        rows = slice(s * CHUNK, (s + 1) * CHUNK)
        row = lax.broadcasted_iota(jnp.int32, (CHUNK, 1), 0) + (i * tm + s * CHUNK)
        return rows, row >= N_DUMMY

    def decays(s):
        rows, real = chunk_rows(s)
        la = jnp.where(real, loga_ref[rows, :], 0.0)
        return la, _tri_sum(tri, la)

    def summary(s, la, sfx):
        rows, real = chunk_rows(s)
        k = jnp.where(real, gla_ref[rows, GLA_QK:2 * GLA_QK], 0.0)
        v = gla_ref[rows, 2 * GLA_QK:2 * GLA_QK + GLA_WIDTH]
        kw = (k * jnp.exp(sfx - la)).astype(BF16)
        u_t = lax.dot_general(v.astype(BF16), kw, (((0,), (0,)), ((), ())),
                              preferred_element_type=F32)
        return jnp.exp(sfx[0:1, :]), u_t

    def readout(s, state, decay, u_t):
        rows, _ = chunk_rows(s)
        q = gla_ref[rows, 0:GLA_QK]
        r = gla_ref[rows, 2 * GLA_QK + GLA_WIDTH:2 * GLA_QK + 2 * GLA_WIDTH]
        state = state * decay + jnp.where(same_head, u_t, 0.0)
        o = lax.dot_general(q.astype(BF16), state.astype(BF16), (((1,), (1,)), ((), ())),
                            preferred_element_type=F32)
        outs = []
        for hd in range(GLA_HEADS):
            oh = o[:, hd * GLA_DV:(hd + 1) * GLA_DV]
            ms = jnp.mean(oh * oh, axis=-1, keepdims=True)
            outs.append(oh * lax.rsqrt(ms + EPS))
        y = jnp.concatenate(outs, axis=1) * gn
        gate = r * (1.0 / (1.0 + jnp.exp(-r)))
        o_ref[0, rows, :] = (y * gate).astype(BF16)
        return state

    state = state_ref[...]
    dec = {s: decays(s) for s in range(min(2, n))}
    summ = {0: summary(0, *dec[0])}
    for s in range(n):
        if s + 2 < n:
            dec[s + 2] = decays(s + 2)
        if s + 1 < n:
            summ[s + 1] = summary(s + 1, *dec.pop(s + 1))
        state = readout(s, state, *summ.pop(s))
        between_chunks(s)
    state_ref[...] = state


def _inproj_kernel(h_ref, g_ref, w_ref, wa_ref, ba_ref, bf_ref, sq_ref, sk_ref, tri_ref, gn_ref,
                   triu_ref, og_ref, qa_ref, ka_ref, va_ref, carry_ref, gla_ref, loga_ref,
                   state_ref):
    i = pl.program_id(1)
    tm = h_ref.shape[1]
    x = h_ref[0]
    ms = jnp.mean(x * x, axis=-1, keepdims=True)
    xn = (x * lax.rsqrt(ms + EPS) * g_ref[...]).astype(BF16)

    @pl.when(i == 0)
    def _():
        carry_ref[...] = jnp.zeros_like(carry_ref)
        state_ref[...] = jnp.zeros_like(state_ref)

    gla_ref[...] = _dot(xn, w_ref[:, _C_GLA:_C_FQ])
    sm = _dot(xn, w_ref[:, _C_SM:_N_MAIN])
    z = _dot(sm.astype(BF16), wa_ref[...]) + ba_ref[...]
    loga_ref[...] = _log_sigmoid(z) * (1.0 / GLA_TAU)

    fox_cols = list(range(_C_FQ, _C_SM, 2 * LANE))
    fox_blocks = []

    def project_fox_block(s):
        if s % 2 == 1 and len(fox_blocks) < len(fox_cols):
            c0 = fox_cols[len(fox_blocks)]
            fox_blocks.append(_dot(xn, w_ref[:, c0:c0 + 2 * LANE]))

    _gla_chunks(i, tm, gla_ref, loga_ref, gn_ref, triu_ref, og_ref, state_ref, project_fox_block)
    while len(fox_blocks) < len(fox_cols):
        project_fox_block(1)
    fq, fk, fv = (jnp.concatenate(fox_blocks[j:j + 2], axis=1) for j in (0, 2, 4))

    lane = lax.broadcasted_iota(jnp.int32, (tm, LANE), 1)
    row = lax.broadcasted_iota(jnp.int32, (tm, LANE), 0) + i * tm
    real = row >= N_DUMMY
    head_lane = lane < FOX_HEADS
    lf = jnp.where(head_lane & real, _log_sigmoid(sm + bf_ref[...]), 0.0)

    carry = carry_ref[0:1, :]
    tri = tri_ref[...]
    blocks = []
    for j in range(tm // CHUNK):
        cs = _tri_sum(tri, lf[j * CHUNK:(j + 1) * CHUNK]) + carry
        carry = cs[CHUNK - 1:CHUNK, :]
        blocks.append(cs)
    carry_ref[...] = jnp.broadcast_to(carry, carry_ref.shape)
    c = jnp.concatenate(blocks, axis=0) * LOG2E
    c = jnp.where(head_lane, jnp.where(real, c, KEY_MASK), 0.0)
    c1, c2, c3 = _split3(c)
    cc = (c1 + pltpu.roll(c2, FOX_HEADS, axis=1)) + (pltpu.roll(c3, 2 * FOX_HEADS, axis=1)
                                                      + jnp.where(lane == 3 * FOX_HEADS, 1.0, 0.0))
    cc = cc.astype(BF16)
    augq = _dot(cc, sq_ref[...])
    augk = _dot(cc, sk_ref[...])

    low = lane < FOX_DH
    ones_col = jnp.where(lane == FOX_DH, 1.0, 0.0)
    for h in range(FOX_HEADS):
        pair = slice((h // 2) * LANE, (h // 2 + 1) * LANE)
        blk = slice(h * LANE, (h + 1) * LANE)
        xq, xk, xv = fq[:, pair], fk[:, pair], fv[:, pair]
        if h % 2:
            xq = pltpu.roll(xq, FOX_DH, axis=1)
            xk = pltpu.roll(xk, FOX_DH, axis=1)
            xv = pltpu.roll(xv, FOX_DH, axis=1)
        qa_ref[0, :, blk] = jnp.where(low, xq * (LOG2E * FOX_DH ** -0.5), augq[:, blk]).astype(BF16)
        ka_ref[0, :, blk] = jnp.where(low, xk, augk[:, blk]).astype(BF16)
        va_ref[0, :, blk] = jnp.where(low, xv, ones_col).astype(BF16)


def _row_tile(lp, candidates=ROW_TILES):
    return next(tm for tm in candidates if lp % tm == 0)


def _inproj(h, g, w_main, wa, ba, bf, sq, sk, tri_lo, gn, tri_up):
    B, LP, D = h.shape
    tm = _row_tile(LP)
    nt = LP // tm
    const = lambda shape: pl.BlockSpec(shape, lambda b, i: (0,) * len(shape),
                                       pipeline_mode=pl.Buffered(1))
    row_blk = lambda n: pl.BlockSpec((1, tm, n), lambda b, i: (b, i, 0))
    return pl.pallas_call(
        _inproj_kernel,
        grid=(B, nt),
        in_specs=[row_blk(D), const((1, D)), const((D, _N_MAIN)), const((LANE, GLA_QK)),
                  const((1, GLA_QK)), const((1, LANE)), const((LANE, FOX_HEADS * LANE)),
                  const((LANE, FOX_HEADS * LANE)), const((CHUNK, CHUNK)), const((1, GLA_WIDTH)),
                  const((CHUNK, CHUNK))],
        out_specs=[row_blk(GLA_WIDTH), row_blk(FOX_HEADS * LANE),
                   row_blk(FOX_HEADS * LANE), row_blk(FOX_HEADS * LANE)],
        out_shape=[jax.ShapeDtypeStruct((B, LP, GLA_WIDTH), BF16),
                   jax.ShapeDtypeStruct((B, LP, FOX_HEADS * LANE), BF16),
                   jax.ShapeDtypeStruct((B, LP, FOX_HEADS * LANE), BF16),
                   jax.ShapeDtypeStruct((B, LP, FOX_HEADS * LANE), BF16)],
        scratch_shapes=[pltpu.VMEM((SUBLANE, LANE), F32), pltpu.VMEM((tm, _C_FQ), F32),
                        pltpu.VMEM((tm, GLA_QK), F32), pltpu.VMEM((GLA_WIDTH, GLA_QK), F32)],
        compiler_params=pltpu.CompilerParams(
            dimension_semantics=("parallel", "arbitrary"), vmem_limit_bytes=VMEM_LIMIT),
        name="inproj",
    )(h, g, w_main, wa, ba, bf, sq, sk, tri_lo, gn, tri_up)


def _fox_kernel(qa_ref, ka_ref, va_ref, gn_ref, o_ref, sa_ref, sb_ref, m_ref, acc_ref):
    t = pl.program_id(2)
    cols = [slice(hh * LANE, (hh + 1) * LANE) for hh in range(FOX_GROUP)]
    nt = (((1,), (1,)), ((), ()))

    def finish(rows, n):
        lane = lax.broadcasted_iota(jnp.int32, (n, LANE), 1)
        low = lane < FOX_DH
        ys = []
        for hh in range(FOX_GROUP):
            acc = acc_ref[hh, 0:n, :]
            sq = acc * acc
            t = jnp.sum(jnp.where(low, sq * (1.0 / FOX_DH), jnp.where(lane == FOX_DH, sq * EPS, 0.0)),
                        axis=-1, keepdims=True)
            ys.append(acc * lax.rsqrt(t) * gn_ref[0, :, cols[hh]])
        for pr in range(FOX_GROUP // 2):
            o_ref[0, rows, pr * LANE:(pr + 1) * LANE] = jnp.where(
                low, ys[2 * pr], pltpu.roll(ys[2 * pr + 1], FOX_DH, axis=1)).astype(BF16)

    def first_block(q_rows, n, causal):
        for hh in range(FOX_GROUP):
            q = qa_ref[0, q_rows, cols[hh]]
            s = lax.dot_general(q, ka_ref[0, 0:FRONT, cols[hh]], nt, preferred_element_type=F32)
            if causal:
                s = jnp.where(lax.broadcasted_iota(jnp.int32, (n, FRONT), 1)
                              <= lax.broadcasted_iota(jnp.int32, (n, FRONT), 0), s, NEG)
            m = jnp.broadcast_to(jnp.max(s, axis=-1, keepdims=True), (n, LANE))
            p = jnp.exp2(s - m)
            m_ref[hh, 0:n, :] = m
            acc_ref[hh, 0:n, :] = _dot(p.astype(BF16), va_ref[0, 0:FRONT, cols[hh]])

    @pl.when(t == 0)
    def _():
        first_block(slice(0, FRONT), FRONT, causal=True)
        finish(slice(0, FRONT), FRONT)

    @pl.when(t > 0)
    def _():
        r0 = pl.multiple_of(FRONT + (t - 1) * TQ, LANE)

        def q_rows(lo, n):
            return pl.ds(pl.multiple_of(r0 + lo, LANE), n)

        def key_rows(c):
            return pl.ds(pl.multiple_of(FRONT + c * TK, LANE), TK)

        def issue_scores(s_ref, c, lo=0, n=TQ):
            for hh in range(FOX_GROUP):
                s_ref[hh, lo:lo + n, :] = lax.dot_general(
                    qa_ref[0, q_rows(lo, n), cols[hh]], ka_ref[0, key_rows(c), cols[hh]], nt,
                    preferred_element_type=F32)

        def update(s_ref, c, lo=0, n=TQ, causal=False):
            rows = slice(lo, lo + n)
            if causal:
                visible = (lax.broadcasted_iota(jnp.int32, (n, TK), 1)
                           <= lax.broadcasted_iota(jnp.int32, (n, TK), 0))
            for hh in range(FOX_GROUP):
                def scores():
                    s = s_ref[hh, rows, :]
                    return jnp.where(visible, s, NEG) if causal else s

                m = m_ref[hh, rows, :]
                m_new = jnp.maximum(
                    m, jnp.broadcast_to(jnp.max(scores(), axis=-1, keepdims=True), (n, LANE)))
                p = jnp.exp2(scores() - jnp.concatenate([m_new] * (TK // LANE), axis=1))
                acc_ref[hh, rows, :] = (jnp.exp2(m - m_new) * acc_ref[hh, rows, :]
                                        + _dot(p.astype(BF16), va_ref[0, key_rows(c), cols[hh]]))
                m_ref[hh, rows, :] = m_new

        first_block(q_rows(0, TQ), TQ, causal=False)
        issue_scores(sa_ref, 0)

        def open_pair(i, carry):
            issue_scores(sb_ref, 2 * i + 1)
            update(sa_ref, 2 * i)
            issue_scores(sa_ref, 2 * i + 2)
            update(sb_ref, 2 * i + 1)
            return carry

        lax.fori_loop(0, t - 1, open_pair, 0)
        c0 = 2 * (t - 1)
        issue_scores(sb_ref, c0 + 1, lo=TK, n=TQ - TK)
        update(sa_ref, c0, lo=0, n=TK, causal=True)
        update(sa_ref, c0, lo=TK, n=TQ - TK)
        update(sb_ref, c0 + 1, lo=TK, n=TQ - TK, causal=True)
        finish(q_rows(0, TQ), TQ)


def _fox(qa, ka, va, gn):
    B, LP, _ = qa.shape
    gw = FOX_GROUP * LANE
    assert TQ == 2 * TK and (LP - FRONT) % TQ == 0
    seq = lambda n: pl.BlockSpec((1, LP, n), lambda b, h, t: (b, 0, h))
    return pl.pallas_call(
        _fox_kernel,
        grid=(B, FOX_HEADS // FOX_GROUP, 1 + (LP - FRONT) // TQ),
        in_specs=[seq(gw), seq(gw), seq(gw), pl.BlockSpec((1, 1, gw), lambda b, h, t: (h, 0, 0))],
        out_specs=seq(gw // 2),
        out_shape=jax.ShapeDtypeStruct((B, LP, FOX_WIDTH), BF16),
        scratch_shapes=[pltpu.VMEM((FOX_GROUP, TQ, TK), F32), pltpu.VMEM((FOX_GROUP, TQ, TK), F32),
                        pltpu.VMEM((FOX_GROUP, TQ, LANE), F32), pltpu.VMEM((FOX_GROUP, TQ, LANE), F32)],
        compiler_params=pltpu.CompilerParams(
            dimension_semantics=("parallel", "parallel", "arbitrary"), vmem_limit_bytes=VMEM_LIMIT),
        name="fox",
    )(qa, ka, va, gn)


def _ffn_kernel(h_ref, og_ref, of_ref, wo_ref, g_ref, wup_ref, cw_ref, cb_ref,
                wd_ref, fin_ref, out_ref, xn_ref, acc_ref, ha_ref, hb_ref, halo_ref, *, final):
    i = pl.program_id(1)
    tm = h_ref.shape[1]

    @pl.when(i == 0)
    def _():
        halo_ref[...] = jnp.zeros_like(halo_ref)

    h1 = h_ref[0] + (_dot(og_ref[0], wo_ref[0:GLA_WIDTH, :]) + _dot(of_ref[0], wo_ref[GLA_WIDTH:, :]))
    acc_ref[...] = h1
    ms = jnp.mean(h1 * h1, axis=-1, keepdims=True)
    row = lax.broadcasted_iota(jnp.int32, (tm, 1), 0) + i * tm
    xn = jnp.where(row >= N_DUMMY, h1 * lax.rsqrt(ms + EPS) * g_ref[...], 0.0)
    xn_ref[...] = xn.astype(BF16)

    def up(buf_ref, j):
        xnb = xn_ref[...]
        for half in range(2):
            col = pl.multiple_of(half * D_FF + j * FF_CHUNK, FF_CHUNK)
            buf_ref[half, SUBLANE:SUBLANE + tm, :] = _dot(xnb, wup_ref[:, pl.ds(col, FF_CHUNK)])

    def conv(buf_ref, half, j):
        buf_ref[half, 0:SUBLANE, :] = halo_ref[half, j]
        halo_ref[half, j] = buf_ref[half, tm:tm + SUBLANE, :]
        w = cw_ref[half, j]
        out = cb_ref[half, j][0:1, :] + buf_ref[half, SUBLANE:SUBLANE + tm, :] * w[2:3, :]
        out = out + buf_ref[half, SUBLANE - 1:SUBLANE - 1 + tm, :] * w[1:2, :]
        return out + buf_ref[half, SUBLANE - 2:SUBLANE - 2 + tm, :] * w[0:1, :]

    def down(buf_ref, j):
        cu = conv(buf_ref, 0, j)
        cg = conv(buf_ref, 1, j)
        a = (cg * (1.0 / (1.0 + jnp.exp(-cg))) * cu).astype(BF16)
        acc_ref[...] += _dot(a, wd_ref[j])

    def pair(p, carry):
        up(hb_ref, 2 * p + 1)
        down(ha_ref, 2 * p)
        up(ha_ref, 2 * p + 2)
        down(hb_ref, 2 * p + 1)
        return carry

    assert N_FF_CHUNKS % 2 == 1
    up(ha_ref, 0)
    lax.fori_loop(0, N_FF_CHUNKS // 2, pair, 0)
    down(ha_ref, N_FF_CHUNKS - 1)
    y = acc_ref[...]
    if final:
        ms = jnp.mean(y * y, axis=-1, keepdims=True)
        y = y * lax.rsqrt(ms + EPS) * fin_ref[...]
    out_ref[0] = y


def _ffn(h, og, of, wo, g, wup, cw, cb, wd, fin, final):
    B, LP, D = h.shape
    tm = _row_tile(LP)
    nt = LP // tm
    resident = lambda shape: pl.BlockSpec(shape, lambda b, i: (0,) * len(shape),
                                          pipeline_mode=pl.Buffered(1))
    row_blk = lambda n: pl.BlockSpec((1, tm, n), lambda b, i: (b, i, 0))
    return pl.pallas_call(
        functools.partial(_ffn_kernel, final=final),
        grid=(B, nt),
        in_specs=[row_blk(D), row_blk(GLA_WIDTH), row_blk(FOX_WIDTH),
                  resident((GLA_WIDTH + FOX_WIDTH, D)), resident((1, D)),
                  resident((D, 2 * D_FF)),
                  resident((2, N_FF_CHUNKS, SUBLANE, FF_CHUNK)),
                  resident((2, N_FF_CHUNKS, SUBLANE, FF_CHUNK)),
                  resident((N_FF_CHUNKS, FF_CHUNK, D)), resident((1, D))],
        out_specs=row_blk(D),
        out_shape=jax.ShapeDtypeStruct((B, LP, D), F32),
        scratch_shapes=[pltpu.VMEM((tm, D), BF16), pltpu.VMEM((tm, D), F32),
                        pltpu.VMEM((2, tm + SUBLANE, FF_CHUNK), F32),
                        pltpu.VMEM((2, tm + SUBLANE, FF_CHUNK), F32),
                        pltpu.VMEM((2, N_FF_CHUNKS, SUBLANE, FF_CHUNK), F32)],
        compiler_params=pltpu.CompilerParams(
            dimension_semantics=("parallel", "arbitrary"), vmem_limit_bytes=VMEM_LIMIT),
        name="ffn",
    )(h, og, of, wo, g, wup, cw, cb, wd, fin)


def _selection_matrices():
    sq = np.zeros((LANE, FOX_HEADS * LANE), np.float32)
    sk = np.zeros((LANE, FOX_HEADS * LANE), np.float32)
    one = 3 * FOX_HEADS
    for h in range(FOX_HEADS):
        base = h * LANE + FOX_DH
        for p in range(3):
            sq[p * FOX_HEADS + h, base + p] = 1.0
            sk[one, base + p] = 1.0
            sq[one, base + 3 + p] = 1.0
            sk[p * FOX_HEADS + h, base + 3 + p] = -1.0
    return jnp.asarray(sq, BF16), jnp.asarray(sk, BF16)


def _chunk_cols(w):
    return jnp.transpose(w.reshape(w.shape[0], N_FF_CHUNKS, FF_CHUNK), (1, 0, 2))


def _pad_rows(w):
    w = jnp.pad(w, ((0, SUBLANE - w.shape[0]), (0, 0)))
    return _chunk_cols(w)


def kernel(x, meta_tokens, attn_norm, w_in, w_alpha_up, b_alpha, b_forget, gla_norm, fox_norm,
           w_out, ffn_norm, w_up, conv_w, conv_b, w_down, final_norm):
    B, S, D = x.shape
    depth = w_in.shape[0]
    assert D == D_MODEL and S % TQ == 0
    meta = jnp.broadcast_to(meta_tokens[None].astype(x.dtype), (B, N_META, D))
    h = jnp.concatenate([jnp.zeros((B, N_DUMMY, D), x.dtype), meta, x], axis=1)

    sq, sk = _selection_matrices()
    tri_lo = jnp.asarray(np.tril(np.ones((CHUNK, CHUNK), np.float32)), BF16)
    tri_up = jnp.asarray(np.triu(np.ones((CHUNK, CHUNK), np.float32)), BF16)
    c = np.cumsum((GLA_QK, GLA_QK, GLA_WIDTH, GLA_WIDTH, GLA_LOWRANK, FOX_WIDTH, FOX_WIDTH,
                   FOX_WIDTH, FOX_HEADS))

    for l in range(depth):
        w = w_in[l]
        w_main = jnp.concatenate(
            [w[:, :c[0]] * GLA_DK ** -0.5, w[:, c[0]:c[3]], w[:, c[4]:c[7]], w[:, c[7]:c[8]],
             w[:, c[3]:c[4]], jnp.zeros((D, LANE - FOX_HEADS - GLA_LOWRANK), w.dtype)],
            axis=1).astype(BF16)
        wa = jnp.zeros((LANE, GLA_QK), F32).at[FOX_HEADS:FOX_HEADS + GLA_LOWRANK].set(
            w_alpha_up[l]).astype(BF16)
        bf = jnp.zeros((1, LANE), F32).at[0, :FOX_HEADS].set(b_forget[l])
        o_gla, qa, ka, va = _inproj(h, attn_norm[l][None], w_main, wa, b_alpha[l][None], bf,
                                    sq, sk, tri_lo, gla_norm[l][None], tri_up)
        fgn = jnp.pad(fox_norm[l].reshape(FOX_HEADS, FOX_DH), ((0, 0), (0, LANE - FOX_DH)))
        o_fox = _fox(qa, ka, va, fgn.reshape(FOX_HEADS // FOX_GROUP, 1, FOX_GROUP * LANE))
        wo = w_out[l].astype(BF16)
        cw = jnp.stack([_pad_rows(conv_w[l][:, :D_FF]), _pad_rows(conv_w[l][:, D_FF:])])
        cb = jnp.stack([_pad_rows(conv_b[l][None, :D_FF]), _pad_rows(conv_b[l][None, D_FF:])])
        h = _ffn(h, o_gla, o_fox, wo, ffn_norm[l][None],
                 w_up[l].astype(BF16), cw, cb,
                 w_down[l].reshape(N_FF_CHUNKS, FF_CHUNK, D).astype(BF16), final_norm[None],
                 final=(l == depth - 1))
    return h[:, FRONT:]
```

```python
import functools

import jax
import jax.numpy as jnp
import numpy as np
from jax import lax
from jax.experimental import pallas as pl
from jax.experimental.pallas import tpu as pltpu

D_MODEL = 1024
N_META = 16
CHUNK = 64
EPS = 1e-6
GLA_HEADS = 4
GLA_DK = 64
GLA_DV = 128
GLA_LOWRANK = 16
GLA_TAU = 16.0
GLA_QK = GLA_HEADS * GLA_DK
GLA_WIDTH = GLA_HEADS * GLA_DV
FOX_HEADS = 8
FOX_DH = 64
FOX_WIDTH = FOX_HEADS * FOX_DH
D_FF = 2816
CONV_W = 3

LANE = 128
SUBLANE = 8
FRONT = 128
N_DUMMY = FRONT - N_META
ROW_TILES = (704, 384, 128)
TQ = 1024
TK = 512
FOX_GROUP = 2
FF_CHUNK = 256
N_FF_CHUNKS = D_FF // FF_CHUNK
LOG2E = 1.4426950408889634
KEY_MASK = 2.0 ** 40
NEG = -1e30
VMEM_LIMIT = 56 * 1024 * 1024

_C_GLA = 0
_C_FQ = 1536
_C_FK = 2048
_C_FV = 2560
_C_SM = 3072
_N_MAIN = 3200

F32 = jnp.float32
BF16 = jnp.bfloat16


def _dot(a, b):
    return jnp.dot(a, b, preferred_element_type=F32)


def _log_sigmoid(x):
    return jnp.minimum(x, 0.0) - jnp.log1p(jnp.exp(-jnp.abs(x)))


def _split3(x):
    p1 = x.astype(BF16).astype(F32)
    r1 = x - p1
    p2 = r1.astype(BF16).astype(F32)
    p3 = (r1 - p2).astype(BF16).astype(F32)
    return p1, p2, p3


def _tri_sum(tri, x):
    p1, p2, p3 = _split3(x)
    return (_dot(tri, p1.astype(BF16)) + _dot(tri, p2.astype(BF16))) + _dot(tri, p3.astype(BF16))


def _gla_chunks(i, tm, gla_ref, loga_ref, gn_ref, tri_ref, o_ref, state_ref, between_chunks):
    tri = tri_ref[...]
    srow = lax.broadcasted_iota(jnp.int32, (GLA_WIDTH, GLA_QK), 0) // GLA_DV
    scol = lax.broadcasted_iota(jnp.int32, (GLA_WIDTH, GLA_QK), 1) // GLA_DK
    same_head = srow == scol
    gn = gn_ref[...]
    n = tm // CHUNK

    def chunk_rows(s):
        rows = slice(s * CHUNK, (s + 1) * CHUNK)
        row = lax.broadcasted_iota(jnp.int32, (CHUNK, 1), 0) + (i * tm + s * CHUNK)
        return rows, row >= N_DUMMY

    def decays(s):
        rows, real = chunk_rows(s)
        la = jnp.where(real, loga_ref[rows, :], 0.0)
        return la, _tri_sum(tri, la)

    def summary(s, la, sfx):
        rows, real = chunk_rows(s)
        k = jnp.where(real, gla_ref[rows, GLA_QK:2 * GLA_QK], 0.0)
        v = gla_ref[rows, 2 * GLA_QK:2 * GLA_QK + GLA_WIDTH]
        kw = (k * jnp.exp(sfx - la)).astype(BF16)
        u_t = lax.dot_general(v.astype(BF16), kw, (((0,), (0,)), ((), ())),
                              preferred_element_type=F32)
        return jnp.exp(sfx[0:1, :]), u_t

    def readout(s, state, decay, u_t):
        rows, _ = chunk_rows(s)
        q = gla_ref[rows, 0:GLA_QK]
        r = gla_ref[rows, 2 * GLA_QK + GLA_WIDTH:2 * GLA_QK + 2 * GLA_WIDTH]
        state = state * decay + jnp.where(same_head, u_t, 0.0)
        o = lax.dot_general(q.astype(BF16), state.astype(BF16), (((1,), (1,)), ((), ())),
                            preferred_element_type=F32)
        outs = []
        for hd in range(GLA_HEADS):
            oh = o[:, hd * GLA_DV:(hd + 1) * GLA_DV]
            ms = jnp.mean(oh * oh, axis=-1, keepdims=True)
            outs.append(oh * lax.rsqrt(ms + EPS))
        y = jnp.concatenate(outs, axis=1) * gn
        gate = r * (1.0 / (1.0 + jnp.exp(-r)))
        o_ref[0, rows, :] = (y * gate).astype(BF16)
        return state

    state = state_ref[...]
    dec = {s: decays(s) for s in range(min(2, n))}
    summ = {0: summary(0, *dec[0])}
    for s in range(n):
        if s + 2 < n:
            dec[s + 2] = decays(s + 2)
        if s + 1 < n:
            summ[s + 1] = summary(s + 1, *dec.pop(s + 1))
        state = readout(s, state, *summ.pop(s))
        between_chunks(s)
    state_ref[...] = state


def _inproj_kernel(h_ref, g_ref, w_ref, wa_ref, ba_ref, bf_ref, sq_ref, sk_ref, tri_ref, gn_ref,
                   triu_ref, og_ref, qa_ref, ka_ref, va_ref, carry_ref, gla_ref, loga_ref,
                   state_ref):
    i = pl.program_id(1)
    tm = h_ref.shape[1]
    x = h_ref[0]
    ms = jnp.mean(x * x, axis=-1, keepdims=True)
    xn = (x * lax.rsqrt(ms + EPS) * g_ref[...]).astype(BF16)

    @pl.when(i == 0)
    def _():
        carry_ref[...] = jnp.zeros_like(carry_ref)
        state_ref[...] = jnp.zeros_like(state_ref)

    gla_ref[...] = _dot(xn, w_ref[:, _C_GLA:_C_FQ])
    sm = _dot(xn, w_ref[:, _C_SM:_N_MAIN])
    z = _dot(sm.astype(BF16), wa_ref[...]) + ba_ref[...]
    loga_ref[...] = _log_sigmoid(z) * (1.0 / GLA_TAU)

    fox_cols = list(range(_C_FQ, _C_SM, 2 * LANE))
    fox_blocks = []

    def project_fox_block(s):
        if s % 2 == 1 and len(fox_blocks) < len(fox_cols):
            c0 = fox_cols[len(fox_blocks)]
            fox_blocks.append(_dot(xn, w_ref[:, c0:c0 + 2 * LANE]))

    _gla_chunks(i, tm, gla_ref, loga_ref, gn_ref, triu_ref, og_ref, state_ref, project_fox_block)
    while len(fox_blocks) < len(fox_cols):
        project_fox_block(1)
    fq, fk, fv = (jnp.concatenate(fox_blocks[j:j + 2], axis=1) for j in (0, 2, 4))

    lane = lax.broadcasted_iota(jnp.int32, (tm, LANE), 1)
    row = lax.broadcasted_iota(jnp.int32, (tm, LANE), 0) + i * tm
    real = row >= N_DUMMY
    head_lane = lane < FOX_HEADS
    lf = jnp.where(head_lane & real, _log_sigmoid(sm + bf_ref[...]), 0.0)

    carry = carry_ref[0:1, :]
    tri = tri_ref[...]
    blocks = []
    for j in range(tm // CHUNK):
        cs = _tri_sum(tri, lf[j * CHUNK:(j + 1) * CHUNK]) + carry
        carry = cs[CHUNK - 1:CHUNK, :]
        blocks.append(cs)
    carry_ref[...] = jnp.broadcast_to(carry, carry_ref.shape)
    c = jnp.concatenate(blocks, axis=0) * LOG2E
    c = jnp.where(head_lane, jnp.where(real, c, KEY_MASK), 0.0)
    c1, c2, c3 = _split3(c)
    cc = (c1 + pltpu.roll(c2, FOX_HEADS, axis=1)) + (pltpu.roll(c3, 2 * FOX_HEADS, axis=1)
                                                      + jnp.where(lane == 3 * FOX_HEADS, 1.0, 0.0))
    cc = cc.astype(BF16)
    augq = _dot(cc, sq_ref[...])
    augk = _dot(cc, sk_ref[...])

    low = lane < FOX_DH
    ones_col = jnp.where(lane == FOX_DH, 1.0, 0.0)
    for h in range(FOX_HEADS):
        pair = slice((h // 2) * LANE, (h // 2 + 1) * LANE)
        blk = slice(h * LANE, (h + 1) * LANE)
        xq, xk, xv = fq[:, pair], fk[:, pair], fv[:, pair]
        if h % 2:
            xq = pltpu.roll(xq, FOX_DH, axis=1)
            xk = pltpu.roll(xk, FOX_DH, axis=1)
            xv = pltpu.roll(xv, FOX_DH, axis=1)
        qa_ref[0, :, blk] = jnp.where(low, xq * (LOG2E * FOX_DH ** -0.5), augq[:, blk]).astype(BF16)
        ka_ref[0, :, blk] = jnp.where(low, jnp.where(real, xk, 0.0), augk[:, blk]).astype(BF16)
        va_ref[0, :, blk] = jnp.where(low, xv, ones_col).astype(BF16)


def _row_tile(lp, candidates=ROW_TILES):
    return next(tm for tm in candidates if lp % tm == 0)


def _inproj(h, g, w_main, wa, ba, bf, sq, sk, tri_lo, gn, tri_up):
    B, LP, D = h.shape
    tm = _row_tile(LP)
    nt = LP // tm
    const = lambda shape: pl.BlockSpec(shape, lambda b, i: (0,) * len(shape),
                                       pipeline_mode=pl.Buffered(1))
    row_blk = lambda n: pl.BlockSpec((1, tm, n), lambda b, i: (b, i, 0))
    return pl.pallas_call(
        _inproj_kernel,
        grid=(B, nt),
        in_specs=[row_blk(D), const((1, D)), const((D, _N_MAIN)), const((LANE, GLA_QK)),
                  const((1, GLA_QK)), const((1, LANE)), const((LANE, FOX_HEADS * LANE)),
                  const((LANE, FOX_HEADS * LANE)), const((CHUNK, CHUNK)), const((1, GLA_WIDTH)),
                  const((CHUNK, CHUNK))],
        out_specs=[row_blk(GLA_WIDTH), row_blk(FOX_HEADS * LANE),
                   row_blk(FOX_HEADS * LANE), row_blk(FOX_HEADS * LANE)],
        out_shape=[jax.ShapeDtypeStruct((B, LP, GLA_WIDTH), BF16),
                   jax.ShapeDtypeStruct((B, LP, FOX_HEADS * LANE), BF16),
                   jax.ShapeDtypeStruct((B, LP, FOX_HEADS * LANE), BF16),
                   jax.ShapeDtypeStruct((B, LP, FOX_HEADS * LANE), BF16)],
        scratch_shapes=[pltpu.VMEM((SUBLANE, LANE), F32), pltpu.VMEM((tm, _C_FQ), F32),
                        pltpu.VMEM((tm, GLA_QK), F32), pltpu.VMEM((GLA_WIDTH, GLA_QK), F32)],
        compiler_params=pltpu.CompilerParams(
            dimension_semantics=("parallel", "arbitrary"), vmem_limit_bytes=VMEM_LIMIT),
        name="inproj",
    )(h, g, w_main, wa, ba, bf, sq, sk, tri_lo, gn, tri_up)


def _fox_kernel(qa_ref, ka_ref, va_ref, gn_ref, o_ref, sa_ref, sb_ref, m_ref, acc_ref):
    t = pl.program_id(2)
    cols = [slice(hh * LANE, (hh + 1) * LANE) for hh in range(FOX_GROUP)]
    nt = (((1,), (1,)), ((), ()))

    def finish(rows, n):
        lane = lax.broadcasted_iota(jnp.int32, (n, LANE), 1)
        low = lane < FOX_DH
        ys = []
        for hh in range(FOX_GROUP):
            acc = acc_ref[hh, 0:n, :]
            sq = acc * acc
            t = jnp.sum(jnp.where(low, sq * (1.0 / FOX_DH), jnp.where(lane == FOX_DH, sq * EPS, 0.0)),
                        axis=-1, keepdims=True)
            ys.append(acc * lax.rsqrt(t) * gn_ref[0, :, cols[hh]])
        for pr in range(FOX_GROUP // 2):
            o_ref[0, rows, pr * LANE:(pr + 1) * LANE] = jnp.where(
                low, ys[2 * pr], pltpu.roll(ys[2 * pr + 1], FOX_DH, axis=1)).astype(BF16)

    def first_block(q_rows, n, causal):
        for hh in range(FOX_GROUP):
            q = qa_ref[0, q_rows, cols[hh]]
            s = lax.dot_general(q, ka_ref[0, 0:FRONT, cols[hh]], nt, preferred_element_type=F32)
            if causal:
                s = jnp.where(lax.broadcasted_iota(jnp.int32, (n, FRONT), 1)
                              <= lax.broadcasted_iota(jnp.int32, (n, FRONT), 0), s, NEG)
            m = jnp.broadcast_to(jnp.max(s, axis=-1, keepdims=True), (n, LANE))
            p = jnp.exp2(s - m)
            m_ref[hh, 0:n, :] = m
            acc_ref[hh, 0:n, :] = _dot(p.astype(BF16), va_ref[0, 0:FRONT, cols[hh]])

    @pl.when(t == 0)
    def _():
        first_block(slice(0, FRONT), FRONT, causal=True)
        finish(slice(0, FRONT), FRONT)

    @pl.when(t > 0)
    def _():
        r0 = pl.multiple_of(FRONT + (t - 1) * TQ, LANE)

        def q_rows(lo, n):
            return pl.ds(pl.multiple_of(r0 + lo, LANE), n)

        def key_rows(c):
            return pl.ds(pl.multiple_of(FRONT + c * TK, LANE), TK)

        def issue_scores(s_ref, c, lo=0, n=TQ):
            for hh in range(FOX_GROUP):
                s_ref[hh, lo:lo + n, :] = lax.dot_general(
                    qa_ref[0, q_rows(lo, n), cols[hh]], ka_ref[0, key_rows(c), cols[hh]], nt,
                    preferred_element_type=F32)

        def update(s_ref, c, lo=0, n=TQ, causal=False):
            rows = slice(lo, lo + n)
            if causal:
                visible = (lax.broadcasted_iota(jnp.int32, (n, TK), 1)
                           <= lax.broadcasted_iota(jnp.int32, (n, TK), 0))
            for hh in range(FOX_GROUP):
                def scores():
                    s = s_ref[hh, rows, :]
                    return jnp.where(visible, s, NEG) if causal else s

                m = m_ref[hh, rows, :]
                m_new = jnp.maximum(
                    m, jnp.broadcast_to(jnp.max(scores(), axis=-1, keepdims=True), (n, LANE)))
                p = jnp.exp2(scores() - jnp.concatenate([m_new] * (TK // LANE), axis=1))
                acc_ref[hh, rows, :] = (jnp.exp2(m - m_new) * acc_ref[hh, rows, :]
                                        + _dot(p.astype(BF16), va_ref[0, key_rows(c), cols[hh]]))
                m_ref[hh, rows, :] = m_new

        first_block(q_rows(0, TQ), TQ, causal=False)
        issue_scores(sa_ref, 0)

        def open_pair(i, carry):
            issue_scores(sb_ref, 2 * i + 1)
            update(sa_ref, 2 * i)
            issue_scores(sa_ref, 2 * i + 2)
            update(sb_ref, 2 * i + 1)
            return carry

        lax.fori_loop(0, t - 1, open_pair, 0)
        c0 = 2 * (t - 1)
        issue_scores(sb_ref, c0 + 1, lo=TK, n=TQ - TK)
        update(sa_ref, c0, lo=0, n=TK, causal=True)
        update(sa_ref, c0, lo=TK, n=TQ - TK)
        update(sb_ref, c0 + 1, lo=TK, n=TQ - TK, causal=True)
        finish(q_rows(0, TQ), TQ)


def _fox(qa, ka, va, gn):
    B, LP, _ = qa.shape
    gw = FOX_GROUP * LANE
    assert TQ == 2 * TK and (LP - FRONT) % TQ == 0
    seq = lambda n: pl.BlockSpec((1, LP, n), lambda b, h, t: (b, 0, h))
    return pl.pallas_call(
        _fox_kernel,
        grid=(B, FOX_HEADS // FOX_GROUP, 1 + (LP - FRONT) // TQ),
        in_specs=[seq(gw), seq(gw), seq(gw), pl.BlockSpec((1, 1, gw), lambda b, h, t: (h, 0, 0))],
        out_specs=seq(gw // 2),
        out_shape=jax.ShapeDtypeStruct((B, LP, FOX_WIDTH), BF16),
        scratch_shapes=[pltpu.VMEM((FOX_GROUP, TQ, TK), F32), pltpu.VMEM((FOX_GROUP, TQ, TK), F32),
                        pltpu.VMEM((FOX_GROUP, TQ, LANE), F32), pltpu.VMEM((FOX_GROUP, TQ, LANE), F32)],
        compiler_params=pltpu.CompilerParams(
            dimension_semantics=("parallel", "parallel", "arbitrary"), vmem_limit_bytes=VMEM_LIMIT),
        name="fox",
    )(qa, ka, va, gn)


def _ffn_kernel(h_ref, og_ref, of_ref, wo_ref, g_ref, wup_ref, cw_ref, cb_ref,
                wd_ref, fin_ref, out_ref, xn_ref, acc_ref, ha_ref, hb_ref, halo_ref, *, final):
    i = pl.program_id(1)
    tm = h_ref.shape[1]

    @pl.when(i == 0)
    def _():
        halo_ref[...] = jnp.zeros_like(halo_ref)

    h1 = h_ref[0] + (_dot(og_ref[0], wo_ref[0:GLA_WIDTH, :]) + _dot(of_ref[0], wo_ref[GLA_WIDTH:, :]))
    acc_ref[...] = h1
    ms = jnp.mean(h1 * h1, axis=-1, keepdims=True)
    row = lax.broadcasted_iota(jnp.int32, (tm, 1), 0) + i * tm
    xn = jnp.where(row >= N_DUMMY, h1 * lax.rsqrt(ms + EPS) * g_ref[...], 0.0)
    xn_ref[...] = xn.astype(BF16)

    def up(buf_ref, j):
        xnb = xn_ref[...]
        for half in range(2):
            col = pl.multiple_of(half * D_FF + j * FF_CHUNK, FF_CHUNK)
            buf_ref[half, SUBLANE:SUBLANE + tm, :] = _dot(xnb, wup_ref[:, pl.ds(col, FF_CHUNK)])

    def conv(buf_ref, half, j):
        buf_ref[half, 0:SUBLANE, :] = halo_ref[half, j]
        halo_ref[half, j] = buf_ref[half, tm:tm + SUBLANE, :]
        w = cw_ref[half, j]
        out = cb_ref[half, j][0:1, :] + buf_ref[half, SUBLANE:SUBLANE + tm, :] * w[2:3, :]
        out = out + buf_ref[half, SUBLANE - 1:SUBLANE - 1 + tm, :] * w[1:2, :]
        return out + buf_ref[half, SUBLANE - 2:SUBLANE - 2 + tm, :] * w[0:1, :]

    def down(buf_ref, j):
        cu = conv(buf_ref, 0, j)
        cg = conv(buf_ref, 1, j)
        a = (cg * (1.0 / (1.0 + jnp.exp(-cg))) * cu).astype(BF16)
        acc_ref[...] += _dot(a, wd_ref[j])

    def pair(p, carry):
        up(hb_ref, 2 * p + 1)
        down(ha_ref, 2 * p)
        up(ha_ref, 2 * p + 2)
        down(hb_ref, 2 * p + 1)
        return carry

    assert N_FF_CHUNKS % 2 == 1
    up(ha_ref, 0)
    lax.fori_loop(0, N_FF_CHUNKS // 2, pair, 0)
    down(ha_ref, N_FF_CHUNKS - 1)
    y = acc_ref[...]
    if final:
        ms = jnp.mean(y * y, axis=-1, keepdims=True)
        y = y * lax.rsqrt(ms + EPS) * fin_ref[...]
    out_ref[0] = y


def _ffn(h, og, of, wo, g, wup, cw, cb, wd, fin, final):
    B, LP, D = h.shape
    tm = _row_tile(LP)
    nt = LP // tm
    resident = lambda shape: pl.BlockSpec(shape, lambda b, i: (0,) * len(shape),
                                          pipeline_mode=pl.Buffered(1))
    row_blk = lambda n: pl.BlockSpec((1, tm, n), lambda b, i: (b, i, 0))
    return pl.pallas_call(
        functools.partial(_ffn_kernel, final=final),
        grid=(B, nt),
        in_specs=[row_blk(D), row_blk(GLA_WIDTH), row_blk(FOX_WIDTH),
                  resident((GLA_WIDTH + FOX_WIDTH, D)), resident((1, D)),
                  resident((D, 2 * D_FF)),
                  resident((2, N_FF_CHUNKS, SUBLANE, FF_CHUNK)),
                  resident((2, N_FF_CHUNKS, SUBLANE, FF_CHUNK)),
                  resident((N_FF_CHUNKS, FF_CHUNK, D)), resident((1, D))],
        out_specs=row_blk(D),
        out_shape=jax.ShapeDtypeStruct((B, LP, D), F32),
        scratch_shapes=[pltpu.VMEM((tm, D), BF16), pltpu.VMEM((tm, D), F32),
                        pltpu.VMEM((2, tm + SUBLANE, FF_CHUNK), F32),
                        pltpu.VMEM((2, tm + SUBLANE, FF_CHUNK), F32),
                        pltpu.VMEM((2, N_FF_CHUNKS, SUBLANE, FF_CHUNK), F32)],
        compiler_params=pltpu.CompilerParams(
            dimension_semantics=("parallel", "arbitrary"), vmem_limit_bytes=VMEM_LIMIT),
        name="ffn",
    )(h, og, of, wo, g, wup, cw, cb, wd, fin)


def _selection_matrices():
    sq = np.zeros((LANE, FOX_HEADS * LANE), np.float32)
    sk = np.zeros((LANE, FOX_HEADS * LANE), np.float32)
    one = 3 * FOX_HEADS
    for h in range(FOX_HEADS):
        base = h * LANE + FOX_DH
        for p in range(3):
            sq[p * FOX_HEADS + h, base + p] = 1.0
            sk[one, base + p] = 1.0
            sq[one, base + 3 + p] = 1.0
            sk[p * FOX_HEADS + h, base + 3 + p] = -1.0
    return jnp.asarray(sq, BF16), jnp.asarray(sk, BF16)


def _chunk_cols(w):
    return jnp.transpose(w.reshape(w.shape[0], N_FF_CHUNKS, FF_CHUNK), (1, 0, 2))


def _pad_rows(w):
    w = jnp.pad(w, ((0, SUBLANE - w.shape[0]), (0, 0)))
    return _chunk_cols(w)


def kernel(x, meta_tokens, attn_norm, w_in, w_alpha_up, b_alpha, b_forget, gla_norm, fox_norm,
           w_out, ffn_norm, w_up, conv_w, conv_b, w_down, final_norm):
    B, S, D = x.shape
    depth = w_in.shape[0]
    assert D == D_MODEL and S % TQ == 0
    meta = jnp.broadcast_to(meta_tokens[None].astype(x.dtype), (B, N_META, D))
    h = jnp.concatenate([jnp.zeros((B, N_DUMMY, D), x.dtype), meta, x], axis=1)

    sq, sk = _selection_matrices()
    tri_lo = jnp.asarray(np.tril(np.ones((CHUNK, CHUNK), np.float32)), BF16)
    tri_up = jnp.asarray(np.triu(np.ones((CHUNK, CHUNK), np.float32)), BF16)
    c = np.cumsum((GLA_QK, GLA_QK, GLA_WIDTH, GLA_WIDTH, GLA_LOWRANK, FOX_WIDTH, FOX_WIDTH,
                   FOX_WIDTH, FOX_HEADS))

    for l in range(depth):
        w = w_in[l]
        w_main = jnp.concatenate(
            [w[:, :c[0]] * GLA_DK ** -0.5, w[:, c[0]:c[3]], w[:, c[4]:c[7]], w[:, c[7]:c[8]],
             w[:, c[3]:c[4]], jnp.zeros((D, LANE - FOX_HEADS - GLA_LOWRANK), w.dtype)],
            axis=1).astype(BF16)
        wa = jnp.zeros((LANE, GLA_QK), F32).at[FOX_HEADS:FOX_HEADS + GLA_LOWRANK].set(
            w_alpha_up[l]).astype(BF16)
        bf = jnp.zeros((1, LANE), F32).at[0, :FOX_HEADS].set(b_forget[l])
        o_gla, qa, ka, va = _inproj(h, attn_norm[l][None], w_main, wa, b_alpha[l][None], bf,
                                    sq, sk, tri_lo, gla_norm[l][None], tri_up)
        fgn = jnp.pad(fox_norm[l].reshape(FOX_HEADS, FOX_DH), ((0, 0), (0, LANE - FOX_DH)))
        o_fox = _fox(qa, ka, va, fgn.reshape(FOX_HEADS // FOX_GROUP, 1, FOX_GROUP * LANE))
        wo = w_out[l].astype(BF16)
        cw = jnp.stack([_pad_rows(conv_w[l][:, :D_FF]), _pad_rows(conv_w[l][:, D_FF:])])
        cb = jnp.stack([_pad_rows(conv_b[l][None, :D_FF]), _pad_rows(conv_b[l][None, D_FF:])])
        h = _ffn(h, o_gla, o_fox, wo, ffn_norm[l][None],
                 w_up[l].astype(BF16), cw, cb,
                 w_down[l].reshape(N_FF_CHUNKS, FF_CHUNK, D).astype(BF16), final_norm[None],
                 final=(l == depth - 1))
    return h[:, FRONT:]
```

```python
import functools

import jax
import jax.numpy as jnp
import numpy as np
from jax import lax
from jax.experimental import pallas as pl
from jax.experimental.pallas import tpu as pltpu

D_MODEL = 1024
N_META = 16
CHUNK = 64
EPS = 1e-6
GLA_HEADS = 4
GLA_DK = 64
GLA_DV = 128
GLA_LOWRANK = 16
GLA_TAU = 16.0
GLA_QK = GLA_HEADS * GLA_DK
GLA_WIDTH = GLA_HEADS * GLA_DV
FOX_HEADS = 8
FOX_DH = 64
FOX_WIDTH = FOX_HEADS * FOX_DH
D_FF = 2816
CONV_W = 3

LANE = 128
SUBLANE = 8
FRONT = 128
N_DUMMY = FRONT - N_META
ROW_TILES = (704, 384, 128)
TQ = 1024
TK = 512
FOX_GROUP = 2
FF_CHUNK = 256
N_FF_CHUNKS = D_FF // FF_CHUNK
LOG2E = 1.4426950408889634
KEY_MASK = 2.0 ** 40
NEG = -1e30
VMEM_LIMIT = 56 * 1024 * 1024

_C_GLA = 0
_C_FQ = 1536
_C_FK = 2048
_C_FV = 2560
_C_SM = 3072
_N_MAIN = 3200

F32 = jnp.float32
BF16 = jnp.bfloat16


def _dot(a, b):
    return jnp.dot(a, b, preferred_element_type=F32)


def _log_sigmoid(x):
    return jnp.minimum(x, 0.0) - jnp.log1p(jnp.exp(-jnp.abs(x)))


def _split3(x):
    p1 = x.astype(BF16).astype(F32)
    r1 = x - p1
    p2 = r1.astype(BF16).astype(F32)
    p3 = (r1 - p2).astype(BF16).astype(F32)
    return p1, p2, p3


def _tri_sum(tri, x):
    p1, p2, p3 = _split3(x)
    return (_dot(tri, p1.astype(BF16)) + _dot(tri, p2.astype(BF16))) + _dot(tri, p3.astype(BF16))


def _gla_chunks(i, tm, gla_ref, loga_ref, gn_ref, tri_ref, o_ref, state_ref, between_chunks):
    tri = tri_ref[...]
    srow = lax.broadcasted_iota(jnp.int32, (GLA_WIDTH, GLA_QK), 0) // GLA_DV
    scol = lax.broadcasted_iota(jnp.int32, (GLA_WIDTH, GLA_QK), 1) // GLA_DK
    same_head = srow == scol
    gn = gn_ref[...]
    n = tm // CHUNK

    def chunk_rows(s):
        rows = slice(s * CHUNK, (s + 1) * CHUNK)
        row = lax.broadcasted_iota(jnp.int32, (CHUNK, 1), 0) + (i * tm + s * CHUNK)
        return rows, row >= N_DUMMY

    def decays(s):
        rows, real = chunk_rows(s)
        la = jnp.where(real, loga_ref[rows, :], 0.0)
        return la, _tri_sum(tri, la)

    def summary(s, la, sfx):
        rows, real = chunk_rows(s)
        k = jnp.where(real, gla_ref[rows, GLA_QK:2 * GLA_QK], 0.0)
        v = gla_ref[rows, 2 * GLA_QK:2 * GLA_QK + GLA_WIDTH]
        kw = (k * jnp.exp(sfx - la)).astype(BF16)
        u_t = lax.dot_general(v.astype(BF16), kw, (((0,), (0,)), ((), ())),
                              preferred_element_type=F32)
        return jnp.exp(sfx[0:1, :]), u_t

    def readout(s, state, decay, u_t):
        rows, _ = chunk_rows(s)
        q = gla_ref[rows, 0:GLA_QK]
        r = gla_ref[rows, 2 * GLA_QK + GLA_WIDTH:2 * GLA_QK + 2 * GLA_WIDTH]
        state = state * decay + jnp.where(same_head, u_t, 0.0)
        o = lax.dot_general(q.astype(BF16), state.astype(BF16), (((1,), (1,)), ((), ())),
                            preferred_element_type=F32)
        outs = []
        for hd in range(GLA_HEADS):
            oh = o[:, hd * GLA_DV:(hd + 1) * GLA_DV]
            ms = jnp.mean(oh * oh, axis=-1, keepdims=True)
            outs.append(oh * lax.rsqrt(ms + EPS))
        y = jnp.concatenate(outs, axis=1) * gn
        gate = r * (1.0 / (1.0 + jnp.exp(-r)))
        o_ref[0, rows, :] = (y * gate).astype(BF16)
        return state

    state = state_ref[...]
    dec = {s: decays(s) for s in range(min(2, n))}
    summ = {0: summary(0, *dec[0])}
    for s in range(n):
        if s + 2 < n:
            dec[s + 2] = decays(s + 2)
        if s + 1 < n:
            summ[s + 1] = summary(s + 1, *dec.pop(s + 1))
        state = readout(s, state, *summ.pop(s))
        between_chunks(s)
    state_ref[...] = state


def _inproj_kernel(h_ref, g_ref, w_ref, wa_ref, ba_ref, bf_ref, sq_ref, sk_ref, tri_ref, gn_ref,
                   triu_ref, og_ref, qa_ref, ka_ref, va_ref, carry_ref, gla_ref, loga_ref,
                   state_ref):
    i = pl.program_id(1)
    tm = h_ref.shape[1]
    x = h_ref[0]
    ms = jnp.mean(x * x, axis=-1, keepdims=True)
    xn = (x * lax.rsqrt(ms + EPS) * g_ref[...]).astype(BF16)

    @pl.when(i == 0)
    def _():
        carry_ref[...] = jnp.zeros_like(carry_ref)
        state_ref[...] = jnp.zeros_like(state_ref)

    gla_ref[...] = _dot(xn, w_ref[:, _C_GLA:_C_FQ])
    sm = _dot(xn, w_ref[:, _C_SM:_N_MAIN])
    z = _dot(sm.astype(BF16), wa_ref[...]) + ba_ref[...]
    loga_ref[...] = _log_sigmoid(z) * (1.0 / GLA_TAU)

    fox_cols = list(range(_C_FQ, _C_SM, 2 * LANE))
    fox_blocks = []

    def project_fox_block(s):
        if s % 2 == 1 and len(fox_blocks) < len(fox_cols):
            c0 = fox_cols[len(fox_blocks)]
            fox_blocks.append(_dot(xn, w_ref[:, c0:c0 + 2 * LANE]))

    _gla_chunks(i, tm, gla_ref, loga_ref, gn_ref, triu_ref, og_ref, state_ref, project_fox_block)
    while len(fox_blocks) < len(fox_cols):
        project_fox_block(1)
    fq, fk, fv = (jnp.concatenate(fox_blocks[j:j + 2], axis=1) for j in (0, 2, 4))

    lane = lax.broadcasted_iota(jnp.int32, (tm, LANE), 1)
    row = lax.broadcasted_iota(jnp.int32, (tm, LANE), 0) + i * tm
    real = row >= N_DUMMY
    head_lane = lane < FOX_HEADS
    lf = jnp.where(head_lane & real, _log_sigmoid(sm + bf_ref[...]), 0.0)

    carry = carry_ref[0:1, :]
    tri = tri_ref[...]
    blocks = []
    for j in range(tm // CHUNK):
        cs = _tri_sum(tri, lf[j * CHUNK:(j + 1) * CHUNK]) + carry
        carry = cs[CHUNK - 1:CHUNK, :]
        blocks.append(cs)
    carry_ref[...] = jnp.broadcast_to(carry, carry_ref.shape)
    c = jnp.concatenate(blocks, axis=0) * LOG2E
    c = jnp.where(head_lane, jnp.where(real, c, KEY_MASK), 0.0)
    c1, c2, c3 = _split3(c)
    cc = (c1 + pltpu.roll(c2, FOX_HEADS, axis=1)) + (pltpu.roll(c3, 2 * FOX_HEADS, axis=1)
                                                      + jnp.where(lane == 3 * FOX_HEADS, 1.0, 0.0))
    cc = cc.astype(BF16)
    augq = _dot(cc, sq_ref[...])
    augk = _dot(cc, sk_ref[...])

    low = lane < FOX_DH
    ones_col = jnp.where(lane == FOX_DH, 1.0, 0.0)
    for h in range(FOX_HEADS):
        pair = slice((h // 2) * LANE, (h // 2 + 1) * LANE)
        blk = slice(h * LANE, (h + 1) * LANE)
        xq, xk, xv = fq[:, pair], fk[:, pair], fv[:, pair]
        if h % 2:
            xq = pltpu.roll(xq, FOX_DH, axis=1)
            xk = pltpu.roll(xk, FOX_DH, axis=1)
            xv = pltpu.roll(xv, FOX_DH, axis=1)
        qa_ref[0, :, blk] = jnp.where(low, xq * (LOG2E * FOX_DH ** -0.5), augq[:, blk]).astype(BF16)
        ka_ref[0, :, blk] = jnp.where(low, jnp.where(real, xk, 0.0), augk[:, blk]).astype(BF16)
        va_ref[0, :, blk] = jnp.where(low, xv, ones_col).astype(BF16)


def _row_tile(lp, candidates=ROW_TILES):
    return next(tm for tm in candidates if lp % tm == 0)


def _inproj(h, g, w_main, wa, ba, bf, sq, sk, tri_lo, gn, tri_up):
    B, LP, D = h.shape
    tm = _row_tile(LP)
    nt = LP // tm
    const = lambda shape: pl.BlockSpec(shape, lambda b, i: (0,) * len(shape),
                                       pipeline_mode=pl.Buffered(1))
    row_blk = lambda n: pl.BlockSpec((1, tm, n), lambda b, i: (b, i, 0))
    return pl.pallas_call(
        _inproj_kernel,
        grid=(B, nt),
        in_specs=[row_blk(D), const((1, D)), const((D, _N_MAIN)), const((LANE, GLA_QK)),
                  const((1, GLA_QK)), const((1, LANE)), const((LANE, FOX_HEADS * LANE)),
                  const((LANE, FOX_HEADS * LANE)), const((CHUNK, CHUNK)), const((1, GLA_WIDTH)),
                  const((CHUNK, CHUNK))],
        out_specs=[row_blk(GLA_WIDTH), row_blk(FOX_HEADS * LANE),
                   row_blk(FOX_HEADS * LANE), row_blk(FOX_HEADS * LANE)],
        out_shape=[jax.ShapeDtypeStruct((B, LP, GLA_WIDTH), BF16),
                   jax.ShapeDtypeStruct((B, LP, FOX_HEADS * LANE), BF16),
                   jax.ShapeDtypeStruct((B, LP, FOX_HEADS * LANE), BF16),
                   jax.ShapeDtypeStruct((B, LP, FOX_HEADS * LANE), BF16)],
        scratch_shapes=[pltpu.VMEM((SUBLANE, LANE), F32), pltpu.VMEM((tm, _C_FQ), F32),
                        pltpu.VMEM((tm, GLA_QK), F32), pltpu.VMEM((GLA_WIDTH, GLA_QK), F32)],
        compiler_params=pltpu.CompilerParams(
            dimension_semantics=("parallel", "arbitrary"), vmem_limit_bytes=VMEM_LIMIT),
        name="inproj",
    )(h, g, w_main, wa, ba, bf, sq, sk, tri_lo, gn, tri_up)


def _fox_kernel(qa_ref, ka_ref, va_ref, gn_ref, o_ref, sa_ref, sb_ref, m_ref, acc_ref):
    n_tiles = (qa_ref.shape[1] - FRONT) // TQ
    cols = [slice(hh * LANE, (hh + 1) * LANE) for hh in range(FOX_GROUP)]
    nt = (((1,), (1,)), ((), ()))

    def finish(rows, n):
        lane = lax.broadcasted_iota(jnp.int32, (n, LANE), 1)
        low = lane < FOX_DH
        ys = []
        for hh in range(FOX_GROUP):
            acc = acc_ref[hh, 0:n, :]
            sq = acc * acc
            t = jnp.sum(jnp.where(low, sq * (1.0 / FOX_DH), jnp.where(lane == FOX_DH, sq * EPS, 0.0)),
                        axis=-1, keepdims=True)
            ys.append(acc * lax.rsqrt(t) * gn_ref[0, :, cols[hh]])
        for pr in range(FOX_GROUP // 2):
            o_ref[0, rows, pr * LANE:(pr + 1) * LANE] = jnp.where(
                low, ys[2 * pr], pltpu.roll(ys[2 * pr + 1], FOX_DH, axis=1)).astype(BF16)

    def first_block(q_rows, n, causal):
        for hh in range(FOX_GROUP):
            q = qa_ref[0, q_rows, cols[hh]]
            s = lax.dot_general(q, ka_ref[0, 0:FRONT, cols[hh]], nt, preferred_element_type=F32)
            if causal:
                s = jnp.where(lax.broadcasted_iota(jnp.int32, (n, FRONT), 1)
                              <= lax.broadcasted_iota(jnp.int32, (n, FRONT), 0), s, NEG)
            m = jnp.broadcast_to(jnp.max(s, axis=-1, keepdims=True), (n, LANE))
            p = jnp.exp2(s - m)
            m_ref[hh, 0:n, :] = m
            acc_ref[hh, 0:n, :] = _dot(p.astype(BF16), va_ref[0, 0:FRONT, cols[hh]])

    first_block(slice(0, FRONT), FRONT, causal=True)
    finish(slice(0, FRONT), FRONT)

    def query_tile(t, carry):
        r0 = pl.multiple_of(FRONT + (t - 1) * TQ, LANE)

        def q_rows(lo, n):
            return pl.ds(pl.multiple_of(r0 + lo, LANE), n)

        def key_rows(c):
            return pl.ds(pl.multiple_of(FRONT + c * TK, LANE), TK)

        def issue_scores(s_ref, c, lo=0, n=TQ):
            for hh in range(FOX_GROUP):
                s_ref[hh, lo:lo + n, :] = lax.dot_general(
                    qa_ref[0, q_rows(lo, n), cols[hh]], ka_ref[0, key_rows(c), cols[hh]], nt,
                    preferred_element_type=F32)

        def update(s_ref, c, lo=0, n=TQ, causal=False):
            rows = slice(lo, lo + n)
            if causal:
                visible = (lax.broadcasted_iota(jnp.int32, (n, TK), 1)
                           <= lax.broadcasted_iota(jnp.int32, (n, TK), 0))
            for hh in range(FOX_GROUP):
                def scores():
                    s = s_ref[hh, rows, :]
                    return jnp.where(visible, s, NEG) if causal else s

                m = m_ref[hh, rows, :]
                m_new = jnp.maximum(
                    m, jnp.broadcast_to(jnp.max(scores(), axis=-1, keepdims=True), (n, LANE)))
                p = jnp.exp2(scores() - jnp.concatenate([m_new] * (TK // LANE), axis=1))
                acc_ref[hh, rows, :] = (jnp.exp2(m - m_new) * acc_ref[hh, rows, :]
                                        + _dot(p.astype(BF16), va_ref[0, key_rows(c), cols[hh]]))
                m_ref[hh, rows, :] = m_new

        first_block(q_rows(0, TQ), TQ, causal=False)
        issue_scores(sa_ref, 0)

        def open_pair(i, carry):
            issue_scores(sb_ref, 2 * i + 1)
            update(sa_ref, 2 * i)
            issue_scores(sa_ref, 2 * i + 2)
            update(sb_ref, 2 * i + 1)
            return carry

        lax.fori_loop(0, t - 1, open_pair, 0)
        c0 = 2 * (t - 1)
        issue_scores(sb_ref, c0 + 1, lo=TK, n=TQ - TK)
        update(sa_ref, c0, lo=0, n=TK, causal=True)
        update(sa_ref, c0, lo=TK, n=TQ - TK)
        update(sb_ref, c0 + 1, lo=TK, n=TQ - TK, causal=True)
        finish(q_rows(0, TQ), TQ)
        return carry

    lax.fori_loop(1, n_tiles + 1, query_tile, 0)


def _fox(qa, ka, va, gn):
    B, LP, _ = qa.shape
    gw = FOX_GROUP * LANE
    assert TQ == 2 * TK and (LP - FRONT) % TQ == 0
    seq = lambda n: pl.BlockSpec((1, LP, n), lambda b, h: (b, 0, h))
    return pl.pallas_call(
        _fox_kernel,
        grid=(B, FOX_HEADS // FOX_GROUP),
        in_specs=[seq(gw), seq(gw), seq(gw), pl.BlockSpec((1, 1, gw), lambda b, h: (h, 0, 0))],
        out_specs=seq(gw // 2),
        out_shape=jax.ShapeDtypeStruct((B, LP, FOX_WIDTH), BF16),
        scratch_shapes=[pltpu.VMEM((FOX_GROUP, TQ, TK), F32), pltpu.VMEM((FOX_GROUP, TQ, TK), F32),
                        pltpu.VMEM((FOX_GROUP, TQ, LANE), F32), pltpu.VMEM((FOX_GROUP, TQ, LANE), F32)],
        compiler_params=pltpu.CompilerParams(
            dimension_semantics=("parallel", "parallel"), vmem_limit_bytes=VMEM_LIMIT),
        name="fox",
    )(qa, ka, va, gn)


def _ffn_kernel(h_ref, og_ref, of_ref, wo_ref, g_ref, wup_ref, cw_ref, cb_ref,
                wd_ref, fin_ref, out_ref, xn_ref, acc_ref, ha_ref, hb_ref, halo_ref, *, final):
    i = pl.program_id(1)
    tm = h_ref.shape[1]

    @pl.when(i == 0)
    def _():
        halo_ref[...] = jnp.zeros_like(halo_ref)

    h1 = h_ref[0] + (_dot(og_ref[0], wo_ref[0:GLA_WIDTH, :]) + _dot(of_ref[0], wo_ref[GLA_WIDTH:, :]))
    acc_ref[...] = h1
    ms = jnp.mean(h1 * h1, axis=-1, keepdims=True)
    row = lax.broadcasted_iota(jnp.int32, (tm, 1), 0) + i * tm
    xn = jnp.where(row >= N_DUMMY, h1 * lax.rsqrt(ms + EPS) * g_ref[...], 0.0)
    xn_ref[...] = xn.astype(BF16)

    def up(buf_ref, j):
        xnb = xn_ref[...]
        for half in range(2):
            col = pl.multiple_of(half * D_FF + j * FF_CHUNK, FF_CHUNK)
            buf_ref[half, SUBLANE:SUBLANE + tm, :] = _dot(xnb, wup_ref[:, pl.ds(col, FF_CHUNK)])

    def conv(buf_ref, half, j):
        buf_ref[half, 0:SUBLANE, :] = halo_ref[half, j]
        halo_ref[half, j] = buf_ref[half, tm:tm + SUBLANE, :]
        w = cw_ref[half, j]
        out = cb_ref[half, j][0:1, :] + buf_ref[half, SUBLANE:SUBLANE + tm, :] * w[2:3, :]
        out = out + buf_ref[half, SUBLANE - 1:SUBLANE - 1 + tm, :] * w[1:2, :]
        return out + buf_ref[half, SUBLANE - 2:SUBLANE - 2 + tm, :] * w[0:1, :]

    def down(buf_ref, j):
        cu = conv(buf_ref, 0, j)
        cg = conv(buf_ref, 1, j)
        a = (cg * (1.0 / (1.0 + jnp.exp(-cg))) * cu).astype(BF16)
        acc_ref[...] += _dot(a, wd_ref[j])

    def pair(p, carry):
        up(hb_ref, 2 * p + 1)
        down(ha_ref, 2 * p)
        up(ha_ref, 2 * p + 2)
        down(hb_ref, 2 * p + 1)
        return carry

    assert N_FF_CHUNKS % 2 == 1
    up(ha_ref, 0)
    lax.fori_loop(0, N_FF_CHUNKS // 2, pair, 0)
    down(ha_ref, N_FF_CHUNKS - 1)
    y = acc_ref[...]
    if final:
        ms = jnp.mean(y * y, axis=-1, keepdims=True)
        y = y * lax.rsqrt(ms + EPS) * fin_ref[...]
    out_ref[0] = y


def _ffn(h, og, of, wo, g, wup, cw, cb, wd, fin, final):
    B, LP, D = h.shape
    tm = _row_tile(LP)
    nt = LP // tm
    resident = lambda shape: pl.BlockSpec(shape, lambda b, i: (0,) * len(shape),
                                          pipeline_mode=pl.Buffered(1))
    row_blk = lambda n: pl.BlockSpec((1, tm, n), lambda b, i: (b, i, 0))
    return pl.pallas_call(
        functools.partial(_ffn_kernel, final=final),
        grid=(B, nt),
        in_specs=[row_blk(D), row_blk(GLA_WIDTH), row_blk(FOX_WIDTH),
                  resident((GLA_WIDTH + FOX_WIDTH, D)), resident((1, D)),
                  resident((D, 2 * D_FF)),
                  resident((2, N_FF_CHUNKS, SUBLANE, FF_CHUNK)),
                  resident((2, N_FF_CHUNKS, SUBLANE, FF_CHUNK)),
                  resident((N_FF_CHUNKS, FF_CHUNK, D)), resident((1, D))],
        out_specs=row_blk(D),
        out_shape=jax.ShapeDtypeStruct((B, LP, D), F32),
        scratch_shapes=[pltpu.VMEM((tm, D), BF16), pltpu.VMEM((tm, D), F32),
                        pltpu.VMEM((2, tm + SUBLANE, FF_CHUNK), F32),
                        pltpu.VMEM((2, tm + SUBLANE, FF_CHUNK), F32),
                        pltpu.VMEM((2, N_FF_CHUNKS, SUBLANE, FF_CHUNK), F32)],
        compiler_params=pltpu.CompilerParams(
            dimension_semantics=("parallel", "arbitrary"), vmem_limit_bytes=VMEM_LIMIT),
        name="ffn",
    )(h, og, of, wo, g, wup, cw, cb, wd, fin)


def _selection_matrices():
    sq = np.zeros((LANE, FOX_HEADS * LANE), np.float32)
    sk = np.zeros((LANE, FOX_HEADS * LANE), np.float32)
    one = 3 * FOX_HEADS
    for h in range(FOX_HEADS):
        base = h * LANE + FOX_DH
        for p in range(3):
            sq[p * FOX_HEADS + h, base + p] = 1.0
            sk[one, base + p] = 1.0
            sq[one, base + 3 + p] = 1.0
            sk[p * FOX_HEADS + h, base + 3 + p] = -1.0
    return jnp.asarray(sq, BF16), jnp.asarray(sk, BF16)


def _chunk_cols(w):
    return jnp.transpose(w.reshape(w.shape[0], N_FF_CHUNKS, FF_CHUNK), (1, 0, 2))


def _pad_rows(w):
    w = jnp.pad(w, ((0, SUBLANE - w.shape[0]), (0, 0)))
    return _chunk_cols(w)


def kernel(x, meta_tokens, attn_norm, w_in, w_alpha_up, b_alpha, b_forget, gla_norm, fox_norm,
           w_out, ffn_norm, w_up, conv_w, conv_b, w_down, final_norm):
    B, S, D = x.shape
    depth = w_in.shape[0]
    assert D == D_MODEL and S % TQ == 0
    meta = jnp.broadcast_to(meta_tokens[None].astype(x.dtype), (B, N_META, D))
    h = jnp.concatenate([jnp.zeros((B, N_DUMMY, D), x.dtype), meta, x], axis=1)

    sq, sk = _selection_matrices()
    tri_lo = jnp.asarray(np.tril(np.ones((CHUNK, CHUNK), np.float32)), BF16)
    tri_up = jnp.asarray(np.triu(np.ones((CHUNK, CHUNK), np.float32)), BF16)
    c = np.cumsum((GLA_QK, GLA_QK, GLA_WIDTH, GLA_WIDTH, GLA_LOWRANK, FOX_WIDTH, FOX_WIDTH,
                   FOX_WIDTH, FOX_HEADS))

    for l in range(depth):
        w = w_in[l]
        w_main = jnp.concatenate(
            [w[:, :c[0]] * GLA_DK ** -0.5, w[:, c[0]:c[3]], w[:, c[4]:c[7]], w[:, c[7]:c[8]],
             w[:, c[3]:c[4]], jnp.zeros((D, LANE - FOX_HEADS - GLA_LOWRANK), w.dtype)],
            axis=1).astype(BF16)
        wa = jnp.zeros((LANE, GLA_QK), F32).at[FOX_HEADS:FOX_HEADS + GLA_LOWRANK].set(
            w_alpha_up[l]).astype(BF16)
        bf = jnp.zeros((1, LANE), F32).at[0, :FOX_HEADS].set(b_forget[l])
        o_gla, qa, ka, va = _inproj(h, attn_norm[l][None], w_main, wa, b_alpha[l][None], bf,
                                    sq, sk, tri_lo, gla_norm[l][None], tri_up)
        fgn = jnp.pad(fox_norm[l].reshape(FOX_HEADS, FOX_DH), ((0, 0), (0, LANE - FOX_DH)))
        o_fox = _fox(qa, ka, va, fgn.reshape(FOX_HEADS // FOX_GROUP, 1, FOX_GROUP * LANE))
        wo = w_out[l].astype(BF16)
        cw = jnp.stack([_pad_rows(conv_w[l][:, :D_FF]), _pad_rows(conv_w[l][:, D_FF:])])
        cb = jnp.stack([_pad_rows(conv_b[l][None, :D_FF]), _pad_rows(conv_b[l][None, D_FF:])])
        h = _ffn(h, o_gla, o_fox, wo, ffn_norm[l][None],
                 w_up[l].astype(BF16), cw, cb,
                 w_down[l].reshape(N_FF_CHUNKS, FF_CHUNK, D).astype(BF16), final_norm[None],
                 final=(l == depth - 1))
    return h[:, FRONT:]
```
